```python
import math
import jax, jax.numpy as jnp
from jax import lax
import numpy as np

D_MODEL = 1024
BATCH = 32
SEQ = 2048
DEPTH = 4
DEC_BATCH = 2
DEC_SEQ = 8192
PAST_LEN = 128

N_MEM = 256
BRANCH_W = 512
N_BRANCH = 3
EPS = 1e-6
GLA_HEADS = 4
GLA_DK = 64
GLA_DV = 128
GLA_RANK = 16
GLA_TAU = 16.0
GLA_CHUNK = 64
GLA_Q = GLA_HEADS * GLA_DK
GLA_V = GLA_HEADS * GLA_DV
DIFF_HEADS = 4
DIFF_HD = 64
DIFF_DV = 2 * DIFF_HD
DIFF_QK = DIFF_HEADS * 2 * DIFF_HD
DIFF_V = DIFF_HEADS * DIFF_DV
Q_BLOCK = 128
CROSS_HEADS = 4
CROSS_HD = 128
CROSS_Q = CROSS_HEADS * CROSS_HD
SPLITS = (GLA_Q, GLA_Q, GLA_V, 2 * GLA_RANK, BRANCH_W,
          DIFF_QK, DIFF_QK, DIFF_V, BRANCH_W,
          CROSS_Q, BRANCH_W, N_BRANCH * D_MODEL)
IN_COLS = GLA_Q * 2 + GLA_V + 2 * GLA_RANK + BRANCH_W + DIFF_QK * 2 + DIFF_V + BRANCH_W + CROSS_Q + BRANCH_W + N_BRANCH * D_MODEL

kernel_name = "hybrid_gla_diffattn_memxattn_encoder"


def rmsnorm(x, g):
    xf = x.astype(jnp.float32)
    y = xf * lax.rsqrt(jnp.mean(xf * xf, axis=-1, keepdims=True) + EPS)
    return (y * g.astype(jnp.float32)).astype(x.dtype)


def gla_scan(q, k, v, log_a):
    B, H, T, DK = q.shape
    DV = v.shape[-1]
    n = T // GLA_CHUNK

    def to_chunks(a):
        return jnp.moveaxis(a.reshape(B, H, n, GLA_CHUNK, a.shape[-1]), 2, 0)

    mask = jnp.tril(jnp.ones((GLA_CHUNK, GLA_CHUNK), dtype=bool))[:, :, None]

    def step(S, inp):
        qc, kc, vc, ac = inp
        b = jnp.cumsum(ac, axis=2)
        o_inter = jnp.einsum('bhtd,bhdv->bhtv', qc * jnp.exp(b), S)
        diff = b[:, :, :, None, :] - b[:, :, None, :, :]
        decay = jnp.exp(jnp.where(mask, diff, -jnp.inf))
        att = jnp.einsum('bhtd,bhsd,bhtsd->bhts', qc, kc, decay)
        o = o_inter + jnp.einsum('bhts,bhsv->bhtv', att, vc)
        b_last = b[:, :, -1:, :]
        S = jnp.exp(b_last[:, :, 0, :])[..., None] * S + jnp.einsum(
            'bhsd,bhsv->bhdv', kc * jnp.exp(b_last - b), vc)
        return S, o

    S0 = jnp.zeros((B, H, DK, DV), jnp.float32)
    _, o = lax.scan(step, S0, (to_chunks(q), to_chunks(k), to_chunks(v), to_chunks(log_a)))
    return jnp.moveaxis(o, 0, 2).reshape(B, H, T, DV)


def gla_branch(q, k, v, glr, w2, b2, g_norm):
    B, T, _ = q.shape

    def heads(a, d):
        return a.reshape(B, T, GLA_HEADS, d).transpose(0, 2, 1, 3).astype(jnp.float32)

    qh = heads(q, GLA_DK) * (GLA_DK ** -0.5)
    kh = heads(k, GLA_DK)
    vh = heads(v, GLA_DV)
    glr = glr.astype(jnp.float32).reshape(B, T, 2, GLA_RANK)
    gate_logits = jnp.einsum('btir,irk->btik', glr, w2.astype(jnp.float32)) + b2.astype(jnp.float32)
    log_a = jax.nn.log_sigmoid(gate_logits) / GLA_TAU
    la_f = heads(log_a[:, :, 0], GLA_DK)
    la_b = heads(log_a[:, :, 1], GLA_DK)
    flip = lambda a: jnp.flip(a, axis=2)
    o_f = gla_scan(qh, kh, vh, la_f)
    o_b = flip(gla_scan(flip(qh), flip(kh), flip(vh), flip(la_b)))
    o = rmsnorm(o_f + o_b, g_norm)
    return o.transpose(0, 2, 1, 3).reshape(B, T, GLA_V)


def diff_attention(q, k, v, lam_params, g_norm, lambda_init):
    B, T, _ = q.shape
    H, HD = DIFF_HEADS, DIFF_HD
    qh = q.reshape(B, T, H, 2, HD).transpose(3, 0, 2, 1, 4).astype(jnp.float32) * (HD ** -0.5)
    kh = k.reshape(B, T, H, 2, HD).transpose(3, 0, 2, 1, 4).astype(jnp.float32)
    vh = v.reshape(B, T, H, DIFF_DV).transpose(0, 2, 1, 3).astype(jnp.float32)
    lp = lam_params.astype(jnp.float32)
    lam = jnp.exp(jnp.sum(lp[0] * lp[1])) - jnp.exp(jnp.sum(lp[2] * lp[3])) + lambda_init
    slopes = 2.0 ** (-8.0 * (jnp.arange(H, dtype=jnp.float32) + 1.0) / H)
    pos = jnp.arange(T, dtype=jnp.float32)
    nb = T // Q_BLOCK
    qb = qh.reshape(2, B, H, nb, Q_BLOCK, HD).transpose(3, 0, 1, 2, 4, 5)
    starts = jnp.arange(nb, dtype=jnp.int32) * Q_BLOCK

    def block(inp):
        qblk, t0 = inp
        tq = t0.astype(jnp.float32) + jnp.arange(Q_BLOCK, dtype=jnp.float32)
        bias = -slopes[:, None, None] * jnp.abs(tq[:, None] - pos[None, :])
        s = jnp.einsum('ibhqd,ibhkd->ibhqk', qblk, kh) + bias
        p = jax.nn.softmax(s, axis=-1)
        a = p[0] - lam * p[1]
        return jnp.einsum('bhqk,bhkv->bhqv', a, vh)

    o = lax.map(block, (qb, starts))
    o = o.transpose(1, 2, 0, 3, 4).reshape(B, H, T, DIFF_DV)
    o = rmsnorm(o, g_norm) * (1.0 - lambda_init)
    return o.transpose(0, 2, 1, 3).reshape(B, T, DIFF_V)


def memory_cross_attention(q, mem, mem_norm_g, w_kv):
    B, T, _ = q.shape
    M = mem.shape[1]
    m = rmsnorm(mem, mem_norm_g)
    kv = jnp.einsum('bmd,de->bme', m, w_kv)
    kh = kv[..., :CROSS_Q].reshape(B, M, CROSS_HEADS, CROSS_HD).astype(jnp.float32)
    vh = kv[..., CROSS_Q:].reshape(B, M, CROSS_HEADS, CROSS_HD).astype(jnp.float32)
    qh = q.reshape(B, T, CROSS_HEADS, CROSS_HD).astype(jnp.float32) * (CROSS_HD ** -0.5)
    p = jax.nn.softmax(jnp.einsum('bthd,bmhd->bhtm', qh, kh), axis=-1)
    o = jnp.einsum('bhtm,bmhd->bthd', p, vh)
    return o.reshape(B, T, CROSS_Q)


def trunk(x, mem, norm_g, w_in, gla_gate_w2, gla_gate_b, gla_norm_g, diff_lambda,
          diff_norm_g, mem_norm_g, w_mem_kv, w_branch, w_out, final_norm_g):
    B, T, D = x.shape
    offsets = [int(o) for o in np.cumsum(np.array(SPLITS))[:-1]]
    for l in range(DEPTH):
        lambda_init = 0.8 - 0.6 * math.exp(-0.3 * l)
        h = rmsnorm(x, norm_g[l])
        proj = jnp.einsum('btd,de->bte', h, w_in[l])
        (g_q, g_k, g_v, g_lr, z_a, d_q, d_k, d_v, z_b, c_q, z_c, gate_in) = jnp.split(proj, offsets, axis=-1)
        o_a = gla_branch(g_q, g_k, g_v, g_lr, gla_gate_w2[l], gla_gate_b[l], gla_norm_g[l]).astype(x.dtype) * jax.nn.silu(z_a)
        o_b = diff_attention(d_q, d_k, d_v, diff_lambda[l], diff_norm_g[l], lambda_init).astype(x.dtype) * jax.nn.silu(z_b)
        o_c = memory_cross_attention(c_q, mem, mem_norm_g[l], w_mem_kv[l]).astype(x.dtype) * jax.nn.silu(z_c)
        branches = jnp.stack([o_a, o_b, o_c], axis=2)
        gates = jax.nn.sigmoid(gate_in.reshape(B, T, N_BRANCH, D))
        merged = jnp.sum(gates * jnp.einsum('btiw,iwd->btid', branches, w_branch[l]), axis=2)
        x = x + jnp.einsum('btd,de->bte', merged, w_out[l])
    return rmsnorm(x, final_norm_g)


def setup_inputs(seed: int = 0) -> dict:
    key = jax.random.key(seed)
    ks = jax.random.split(key, 17)
    f32 = jnp.float32
    nrm = lambda k, s: jax.random.normal(k, s, f32)
    return {
        "x_prompt": nrm(ks[0], (BATCH, SEQ, D_MODEL)),
        "x_sample": nrm(ks[1], (DEC_BATCH, DEC_SEQ, D_MODEL)),
        "mem_prompt": nrm(ks[2], (BATCH, N_MEM, D_MODEL)),
        "mem_sample": nrm(ks[3], (DEC_BATCH, N_MEM, D_MODEL)),
        "norm_g": 1.0 + 0.02 * nrm(ks[4], (DEPTH, D_MODEL)),
        "w_in": nrm(ks[5], (DEPTH, D_MODEL, IN_COLS)) * D_MODEL ** -0.5,
        "gla_gate_w2": nrm(ks[6], (DEPTH, 2, GLA_RANK, GLA_Q)) * GLA_RANK ** -0.5,
        "gla_gate_b": 0.1 * nrm(ks[7], (DEPTH, 2, GLA_Q)),
        "gla_norm_g": 1.0 + 0.02 * nrm(ks[8], (DEPTH, GLA_DV)),
        "diff_lambda": 0.1 * nrm(ks[9], (DEPTH, 4, DIFF_HD)),
        "diff_norm_g": 1.0 + 0.02 * nrm(ks[10], (DEPTH, DIFF_DV)),
        "mem_norm_g": 1.0 + 0.02 * nrm(ks[11], (DEPTH, D_MODEL)),
        "w_mem_kv": nrm(ks[12], (DEPTH, D_MODEL, 2 * CROSS_Q)) * D_MODEL ** -0.5,
        "w_branch": nrm(ks[13], (DEPTH, N_BRANCH, BRANCH_W, D_MODEL)) * BRANCH_W ** -0.5,
        "w_out": nrm(ks[14], (DEPTH, D_MODEL, D_MODEL)) * D_MODEL ** -0.5,
        "final_norm_g": 1.0 + 0.02 * nrm(ks[15], (D_MODEL,)),
    }


def reference(x_prompt, x_sample, mem_prompt, mem_sample, norm_g, w_in, gla_gate_w2, gla_gate_b,
              gla_norm_g, diff_lambda, diff_norm_g, mem_norm_g, w_mem_kv, w_branch, w_out, final_norm_g):
    y_prompt = trunk(x_prompt, mem_prompt, norm_g, w_in, gla_gate_w2, gla_gate_b, gla_norm_g, diff_lambda,
                     diff_norm_g, mem_norm_g, w_mem_kv, w_branch, w_out, final_norm_g)
    y_sample = trunk(x_sample, mem_sample, norm_g, w_in, gla_gate_w2, gla_gate_b, gla_norm_g, diff_lambda,
                     diff_norm_g, mem_norm_g, w_mem_kv, w_branch, w_out, final_norm_g)
    return (y_prompt, y_sample)
```

```python
import functools
import math

import numpy as np
import jax
import jax.numpy as jnp
from jax import lax
from jax.experimental import pallas as pl
from jax.experimental.pallas import tpu as pltpu

F32 = jnp.float32
BF16 = jnp.bfloat16

D_MODEL = 1024
DEPTH = 4
N_MEM = 256
BRANCH_W = 512
N_BRANCH = 3
EPS = 1e-6
GLA_HEADS = 4
GLA_DK = 64
GLA_DV = 128
GLA_RANK = 16
GLA_TAU = 16.0
GLA_Q = GLA_HEADS * GLA_DK
GLA_V = GLA_HEADS * GLA_DV
DIFF_HEADS = 4
DIFF_HD = 64
DIFF_DV = 2 * DIFF_HD
DIFF_QK = DIFF_HEADS * 2 * DIFF_HD
DIFF_V = DIFF_HEADS * DIFF_DV
CROSS_HEADS = 4
CROSS_HD = 128
CROSS_Q = CROSS_HEADS * CROSS_HD
SPLITS = (GLA_Q, GLA_Q, GLA_V, 2 * GLA_RANK, BRANCH_W,
          DIFF_QK, DIFF_QK, DIFF_V, BRANCH_W,
          CROSS_Q, BRANCH_W, N_BRANCH * D_MODEL)

LANES = 128
VMEM_LIMIT = 56 * 1024 * 1024
GLA_CHUNK = 64
GLA_LEVELS = 6
GLR_PAD = LANES
GLA_IN = 2 * GLA_Q + GLA_V + GLR_PAD

_NT = (((1,), (1,)), ((), ()))
_TN = (((0,), (0,)), ((), ()))


def _cparams(sem):
    return pltpu.CompilerParams(dimension_semantics=sem, vmem_limit_bytes=VMEM_LIMIT)


def _norm_matmul_kernel(x_ref, g_ref, w_ref, o_ref, h_ref):
    @pl.when(pl.program_id(1) == 0)
    def _():
        x = x_ref[...]
        y = x * lax.rsqrt(jnp.mean(x * x, axis=-1, keepdims=True) + EPS)
        h_ref[...] = (y * g_ref[...]).astype(BF16)

    o_ref[...] = jnp.dot(h_ref[...], w_ref[...], preferred_element_type=F32).astype(o_ref.dtype)


def norm_matmul(x2d, g, w_bf16, out_dtype, tm, tn):
    m, d = x2d.shape
    n = w_bf16.shape[1]
    return pl.pallas_call(
        _norm_matmul_kernel,
        out_shape=jax.ShapeDtypeStruct((m, n), out_dtype),
        grid=(m // tm, n // tn),
        in_specs=[pl.BlockSpec((tm, d), lambda i, j: (i, 0)),
                  pl.BlockSpec((1, d), lambda i, j: (0, 0)),
                  pl.BlockSpec((d, tn), lambda i, j: (0, j))],
        out_specs=pl.BlockSpec((tm, tn), lambda i, j: (i, j)),
        scratch_shapes=[pltpu.VMEM((tm, d), BF16)],
        compiler_params=_cparams(("parallel", "arbitrary")),
        name="norm_matmul",
    )(x2d, g.reshape(1, d).astype(F32), w_bf16)


def _gla_constants(reverse):
    c = GLA_CHUNK
    t = np.arange(c)[:, None]
    u = np.arange(c)[None, :]
    blocks = [(u <= t).astype(np.float32), (u > t).astype(np.float32)]
    masks = [(t == u).astype(np.float32)]
    for lvl in range(1, GLA_LEVELS + 1):
        bs, half = 1 << lvl, 1 << (lvl - 1)
        start = (t // bs) * bs
        boundary = start + half - 1
        right = (t % bs) >= half
        m = np.where(right, (u > boundary) & (u <= t), (u > t) & (u <= boundary))
        blocks.append(m.astype(np.float32))
        s = np.arange(c)[None, :]
        same = (s // bs) == (t // bs)
        masks.append((same & right & ((s % bs) < half)).astype(np.float32))
    mall = np.concatenate(blocks, axis=0)
    mask = np.stack(masks, axis=0)
    if reverse:
        mall = mall.reshape(-1, c, c)[:, ::-1, ::-1].reshape(-1, c)
        mask = mask[:, ::-1, ::-1]
    mask = np.concatenate([mask, mask], axis=-1)
    return jnp.asarray(mall, BF16), jnp.asarray(np.ascontiguousarray(mask), F32)


def _gla_kernel(g_ref, mall_ref, mask_ref, w2_ref, b2_ref, o_ref, st_ref, *, reverse, n_chunks):
    c = GLA_CHUNK

    @pl.when(pl.program_id(1) == 0)
    def _():
        st_ref[...] = jnp.zeros_like(st_ref)

    lane = lax.broadcasted_iota(jnp.int32, (c, LANES), 1)
    lane_v = lax.broadcasted_iota(jnp.int32, (c, 2 * GLA_DV), 1)
    lo, hi = lane < GLA_DK, lane >= GLA_DK
    vlo, vhi = lane_v < GLA_DV, lane_v >= GLA_DV
    srow = lax.broadcasted_iota(jnp.int32, (2 * GLA_DV, LANES), 0)
    slane = lax.broadcasted_iota(jnp.int32, (2 * GLA_DV, LANES), 1)
    state_mask = (srow < GLA_DV) == (slane < GLA_DK)
    end_row = 0 if reverse else c - 1

    def chunk(ci, carry):
        idx = (n_chunks - 1 - ci) if reverse else ci
        rows = pl.ds(pl.multiple_of(idx * c, c), c)
        glr = g_ref[0, rows, 2 * GLA_Q + GLA_V:GLA_IN]
        logits = jnp.dot(glr.astype(BF16), w2_ref[...], preferred_element_type=F32) + b2_ref[...]
        log_a = (jnp.minimum(logits, 0.0) - jnp.log1p(jnp.exp(-jnp.abs(logits)))) * (1.0 / GLA_TAU)
        e_all = jnp.exp(jnp.dot(mall_ref[...], log_a.astype(BF16), preferred_element_type=F32))
        for p in range(GLA_HEADS // 2):
            lanes = slice(p * LANES, (p + 1) * LANES)
            q = g_ref[0, rows, p * LANES:(p + 1) * LANES] * (GLA_DK ** -0.5)
            k = g_ref[0, rows, GLA_Q + p * LANES:GLA_Q + (p + 1) * LANES]
            v = g_ref[0, rows, 2 * GLA_Q + p * 2 * GLA_DV:2 * GLA_Q + (p + 1) * 2 * GLA_DV]
            v2 = jnp.concatenate([jnp.where(vlo, v, 0.0), jnp.where(vhi, v, 0.0)], axis=0).astype(BF16)
            e_start = e_all[0:c, lanes]
            e_end = e_all[c:2 * c, lanes]
            st = st_ref[p]
            o = lax.dot_general((q * e_start).astype(BF16), st.astype(BF16), _NT,
                                preferred_element_type=F32)
            att = jnp.zeros((c, LANES), F32)
            for lvl in range(GLA_LEVELS + 1):
                if lvl == 0:
                    qt, kt = q, k
                else:
                    e = e_all[(lvl + 1) * c:(lvl + 2) * c, lanes]
                    qt, kt = q * e, k * e
                k2 = jnp.concatenate([jnp.where(lo, kt, 0.0), jnp.where(hi, kt, 0.0)], axis=0)
                a = lax.dot_general(qt.astype(BF16), k2.astype(BF16), _NT, preferred_element_type=F32)
                att = att + a * mask_ref[lvl]
            o = o + jnp.dot(att.astype(BF16), v2, preferred_element_type=F32)
            o_ref[0, rows, p * 2 * GLA_DV:(p + 1) * 2 * GLA_DV] = o
            upd = lax.dot_general(v.astype(BF16), (k * e_end).astype(BF16), _TN,
                                  preferred_element_type=F32)
            decay = e_start[end_row:end_row + 1, :]
            st_ref[p] = st * decay + jnp.where(state_mask, upd, 0.0)
        return carry

    lax.fori_loop(0, n_chunks, chunk, 0)


def gla(g_in, w2pad, b2, reverse, tb):
    b, t, _ = g_in.shape
    nblk = t // tb
    mall, mask = _gla_constants(reverse)
    tmap = (lambda bi, i: (bi, nblk - 1 - i, 0)) if reverse else (lambda bi, i: (bi, i, 0))
    const2 = lambda bi, i: (0, 0)
    return pl.pallas_call(
        functools.partial(_gla_kernel, reverse=reverse, n_chunks=tb // GLA_CHUNK),
        out_shape=jax.ShapeDtypeStruct((b, t, GLA_V), F32),
        grid=(b, nblk),
        in_specs=[pl.BlockSpec((1, tb, GLA_IN), tmap),
                  pl.BlockSpec(mall.shape, const2),
                  pl.BlockSpec(mask.shape, lambda bi, i: (0, 0, 0)),
                  pl.BlockSpec(w2pad.shape, const2),
                  pl.BlockSpec(b2.shape, const2)],
        out_specs=pl.BlockSpec((1, tb, GLA_V), tmap),
        scratch_shapes=[pltpu.VMEM((GLA_HEADS // 2, 2 * GLA_DV, LANES), F32)],
        compiler_params=_cparams(("parallel", "arbitrary")),
        name="gla_bwd" if reverse else "gla_fwd",
    )(g_in, mall, mask, w2pad, b2)


def _alibi_constants(t, tq):
    slopes = 2.0 ** (-8.0 * (np.arange(DIFF_HEADS, dtype=np.float64) + 1.0) / DIFF_HEADS)
    pos = np.arange(t)
    hi_part = (pos // LANES) * LANES
    lo_part = pos % LANES
    kaug = np.zeros((DIFF_HEADS, t, LANES), np.float32)
    qaug = np.zeros((DIFF_HEADS, t, LANES), np.float32)
    for h in range(DIFF_HEADS):
        kaug[h, :, 0] = -slopes[h] * hi_part
        kaug[h, :, 1] = -slopes[h] * lo_part
        kaug[h, :, 2] = 1.0
        kaug[h, :, 3] = 1.0
        qaug[h, :, 0] = 1.0
        qaug[h, :, 1] = 1.0
        qaug[h, :, 2] = slopes[h] * hi_part
        qaug[h, :, 3] = slopes[h] * lo_part
    i = np.arange(tq)
    dtab = -slopes[:, None, None] * np.abs(i[:, None] - i[None, :])[None]
    return jnp.asarray(kaug, BF16), jnp.asarray(qaug, BF16), jnp.asarray(dtab, F32)


def _diff_kernel(q_ref, k_ref, v_ref, kaug_ref, qaug_ref, dtab_ref, lp_ref, gn_ref, o_ref,
                 kp_ref, vp_ref, acc_ref, *, tq, nk, lambda_init):
    qi = pl.program_id(2)

    @pl.when(qi == 0)
    def _():
        kp_ref[:, 0:LANES] = k_ref[0]
        kp_ref[:, LANES:2 * LANES] = kaug_ref[0]
        vp_ref[:, 0:LANES] = v_ref[0]
        vp_ref[:, LANES:2 * LANES] = jnp.ones((vp_ref.shape[0], LANES), BF16)

    lane = lax.broadcasted_iota(jnp.int32, (tq, LANES), 1)
    qs = q_ref[0] * (DIFF_HD ** -0.5)
    qh = (jnp.where(lane < DIFF_HD, qs, 0.0).astype(BF16), jnp.where(lane >= DIFF_HD, qs, 0.0).astype(BF16))
    qa = qaug_ref[0]
    acc_ref[...] = jnp.zeros_like(acc_ref)

    def step(kj, m, aug, bias):
        rows = pl.ds(pl.multiple_of(kj * tq, tq), tq)
        kblk = kp_ref[rows, :]
        vblk = vp_ref[rows, :]
        new_m = []
        for i in range(2):
            s = lax.dot_general(jnp.concatenate([qh[i], aug], axis=1), kblk, _NT, preferred_element_type=F32)
            if bias is not None:
                s = s + bias
            m_new = jnp.maximum(m[i], jnp.max(s, axis=-1, keepdims=True))
            p = jnp.exp(s - m_new)
            acc_ref[i] = acc_ref[i] * jnp.exp(m[i] - m_new) + jnp.dot(p.astype(BF16), vblk, preferred_element_type=F32)
            new_m.append(m_new)
        return tuple(new_m)

    m0 = jnp.full((tq, 1), -jnp.inf, F32)
    m = step(qi, (m0, m0), jnp.zeros((tq, LANES), BF16), dtab_ref[0])
    m = lax.fori_loop(0, qi, lambda kj, mm: step(kj, mm, -qa, None), m)
    m = lax.fori_loop(qi + 1, nk, lambda kj, mm: step(kj, mm, qa, None), m)

    lp = lp_ref[...]
    lam = (jnp.exp(jnp.sum(lp[0:1] * lp[1:2], axis=-1, keepdims=True))
           - jnp.exp(jnp.sum(lp[2:3] * lp[3:4], axis=-1, keepdims=True)) + lambda_init)
    a1, a2 = acc_ref[0], acc_ref[1]
    o = a1[:, 0:LANES] / a1[:, LANES:2 * LANES] - lam * (a2[:, 0:LANES] / a2[:, LANES:2 * LANES])
    y = o * lax.rsqrt(jnp.mean(o * o, axis=-1, keepdims=True) + EPS)
    o_ref[0] = (y * gn_ref[...]) * (1.0 - lambda_init)


def diff_attn(qkv, lam_params, g_norm, lambda_init, tq):
    b, t, _ = qkv.shape
    nq = t // tq
    kaug, qaug, dtab = _alibi_constants(t, tq)
    h_ = DIFF_HEADS
    return pl.pallas_call(
        functools.partial(_diff_kernel, tq=tq, nk=nq, lambda_init=lambda_init),
        out_shape=jax.ShapeDtypeStruct((b, t, DIFF_V), F32),
        grid=(b, h_, nq),
        in_specs=[pl.BlockSpec((1, tq, LANES), lambda bi, h, qi: (bi, qi, h)),
                  pl.BlockSpec((1, t, LANES), lambda bi, h, qi: (bi, 0, h_ + h)),
                  pl.BlockSpec((1, t, LANES), lambda bi, h, qi: (bi, 0, 2 * h_ + h)),
                  pl.BlockSpec((1, t, LANES), lambda bi, h, qi: (h, 0, 0)),
                  pl.BlockSpec((1, tq, LANES), lambda bi, h, qi: (h, qi, 0)),
                  pl.BlockSpec((1, tq, tq), lambda bi, h, qi: (h, 0, 0)),
                  pl.BlockSpec((4, DIFF_HD), lambda bi, h, qi: (0, 0)),
                  pl.BlockSpec((1, DIFF_DV), lambda bi, h, qi: (0, 0))],
        out_specs=pl.BlockSpec((1, tq, LANES), lambda bi, h, qi: (bi, qi, h)),
        scratch_shapes=[pltpu.VMEM((t, 2 * LANES), BF16),
                        pltpu.VMEM((t, 2 * LANES), BF16),
                        pltpu.VMEM((2, tq, 2 * LANES), F32)],
        compiler_params=_cparams(("parallel", "parallel", "arbitrary")),
        name="diff_attn",
    )(qkv, qkv, qkv, kaug, qaug, dtab, lam_params.astype(F32), g_norm.reshape(1, DIFF_DV).astype(F32))


def _cross_kernel(q_ref, kv_ref, o_ref):
    for h in range(CROSS_HEADS):
        qh = q_ref[0, :, h * CROSS_HD:(h + 1) * CROSS_HD]
        kh = kv_ref[0, :, h * CROSS_HD:(h + 1) * CROSS_HD]
        vh = kv_ref[0, :, CROSS_Q + h * CROSS_HD:CROSS_Q + (h + 1) * CROSS_HD]
        s = lax.dot_general(qh, kh, _NT, preferred_element_type=F32) * (CROSS_HD ** -0.5)
        p = jnp.exp(s - jnp.max(s, axis=-1, keepdims=True))
        l = jnp.sum(p, axis=-1, keepdims=True)
        o = jnp.dot(p.astype(BF16), vh, preferred_element_type=F32)
        o_ref[0, :, h * CROSS_HD:(h + 1) * CROSS_HD] = o / l


def cross_attn(q, kv, tq):
    b, t, _ = q.shape
    return pl.pallas_call(
        _cross_kernel,
        out_shape=jax.ShapeDtypeStruct((b, t, CROSS_Q), F32),
        grid=(b, t // tq),
        in_specs=[pl.BlockSpec((1, tq, CROSS_Q), lambda bi, i: (bi, i, 0)),
                  pl.BlockSpec((1, N_MEM, 2 * CROSS_Q), lambda bi, i: (bi, 0, 0))],
        out_specs=pl.BlockSpec((1, tq, CROSS_Q), lambda bi, i: (bi, i, 0)),
        compiler_params=_cparams(("parallel", "parallel")),
        name="cross_attn",
    )(q, kv)


def _silu(z):
    return z / (1.0 + jnp.exp(-z))


def _sigmoid(z):
    return 1.0 / (1.0 + jnp.exp(-z))


def _merge_kernel(x_ref, of_ref, ob_ref, od_ref, oc_ref, gt_ref, gn_ref, wb_ref, wo_ref, fg_ref, o_ref,
                  *, final):
    og = of_ref[...] + ob_ref[...]
    parts = []
    for h in range(GLA_HEADS):
        oh = og[:, h * GLA_DV:(h + 1) * GLA_DV]
        parts.append(oh * lax.rsqrt(jnp.mean(oh * oh, axis=-1, keepdims=True) + EPS) * gn_ref[...])
    branches = (jnp.concatenate(parts, axis=1), od_ref[...], oc_ref[...])
    merged = None
    for i in range(N_BRANCH):
        z = gt_ref[:, i * BRANCH_W:(i + 1) * BRANCH_W]
        a = (branches[i] * _silu(z)).astype(BF16)
        y = jnp.dot(a, wb_ref[i], preferred_element_type=F32)
        gate = _sigmoid(gt_ref[:, N_BRANCH * BRANCH_W + i * D_MODEL:N_BRANCH * BRANCH_W + (i + 1) * D_MODEL])
        merged = gate * y if merged is None else merged + gate * y
    x = x_ref[...] + jnp.dot(merged.astype(BF16), wo_ref[...], preferred_element_type=F32)
    if final:
        x = (x * lax.rsqrt(jnp.mean(x * x, axis=-1, keepdims=True) + EPS)) * fg_ref[...]
    o_ref[...] = x


def merge(x2d, of, ob, od, oc, gates, gla_norm_g, wb, wo, final_g, final, tm):
    m, d = x2d.shape
    ng = gates.shape[1]
    row = lambda w: pl.BlockSpec((tm, w), lambda i: (i, 0))
    return pl.pallas_call(
        functools.partial(_merge_kernel, final=final),
        out_shape=jax.ShapeDtypeStruct((m, d), F32),
        grid=(m // tm,),
        in_specs=[row(d), row(BRANCH_W), row(BRANCH_W), row(BRANCH_W), row(BRANCH_W), row(ng),
                  pl.BlockSpec((1, GLA_DV), lambda i: (0, 0)),
                  pl.BlockSpec((N_BRANCH, BRANCH_W, d), lambda i: (0, 0, 0)),
                  pl.BlockSpec((d, d), lambda i: (0, 0)),
                  pl.BlockSpec((1, d), lambda i: (0, 0))],
        out_specs=row(d),
        compiler_params=_cparams(("parallel",)),
        name="merge_final" if final else "merge",
    )(x2d, of, ob, od, oc, gates, gla_norm_g.reshape(1, GLA_DV).astype(F32), wb, wo,
      final_g.reshape(1, d).astype(F32))


def _split_w_in(w_in_l):
    offs = np.concatenate([[0], np.cumsum(np.array(SPLITS))])
    col = lambda i: w_in_l[:, int(offs[i]):int(offs[i + 1])]
    g_q, g_k, g_v, g_lr, z_a, d_q, d_k, d_v, z_b, c_q, z_c, gate_in = (col(i) for i in range(len(SPLITS)))
    lr_pad = jnp.pad(g_lr, ((0, 0), (0, GLR_PAD - 2 * GLA_RANK)))
    w_gla = jnp.concatenate([g_q, g_k, g_v, lr_pad], axis=1).astype(BF16)
    w_diff = jnp.concatenate([d_q, d_k, d_v], axis=1).astype(BF16)
    w_gates = jnp.concatenate([z_a, z_b, z_c, gate_in], axis=1).astype(BF16)
    return w_gla, w_diff, c_q.astype(BF16), w_gates


def _gate_weights(w2_l, b2_l, direction):
    w2pad = jnp.zeros((GLR_PAD, GLA_Q), F32)
    w2pad = w2pad.at[direction * GLA_RANK:(direction + 1) * GLA_RANK].set(w2_l[direction].astype(F32))
    return w2pad.astype(BF16), b2_l[direction].reshape(1, GLA_Q).astype(F32)


def _tiles(t):
    return dict(tm=512, tb=min(t, 512), tq=256, tc=min(t, 512))


def _trunk(x, mem, norm_g, w_in, gla_gate_w2, gla_gate_b, gla_norm_g, diff_lambda, diff_norm_g,
           mem_norm_g, w_mem_kv, w_branch, w_out, final_norm_g):
    b, t, d = x.shape
    tl = _tiles(t)
    x2d = x.reshape(b * t, d)
    mem2d = mem.reshape(b * N_MEM, d)
    for l in range(DEPTH):
        lambda_init = 0.8 - 0.6 * math.exp(-0.3 * l)
        w_gla, w_diff, w_cq, w_gates = _split_w_in(w_in[l])
        g_in = norm_matmul(x2d, norm_g[l], w_gla, F32, tl["tm"], GLA_IN).reshape(b, t, GLA_IN)
        qkv = norm_matmul(x2d, norm_g[l], w_diff, BF16, tl["tm"], 3 * DIFF_QK).reshape(b, t, 3 * DIFF_QK)
        cq = norm_matmul(x2d, norm_g[l], w_cq, BF16, tl["tm"], CROSS_Q).reshape(b, t, CROSS_Q)
        gates = norm_matmul(x2d, norm_g[l], w_gates, F32, tl["tm"], 1536)
        kv = norm_matmul(mem2d, mem_norm_g[l], w_mem_kv[l].astype(BF16), BF16, N_MEM, 2 * CROSS_Q)
        kv = kv.reshape(b, N_MEM, 2 * CROSS_Q)
        o_f = gla(g_in, *_gate_weights(gla_gate_w2[l], gla_gate_b[l], 0), False, tl["tb"])
        o_b = gla(g_in, *_gate_weights(gla_gate_w2[l], gla_gate_b[l], 1), True, tl["tb"])
        o_d = diff_attn(qkv, diff_lambda[l], diff_norm_g[l], lambda_init, tl["tq"])
        o_c = cross_attn(cq, kv, tl["tc"])
        x2d = merge(x2d, o_f.reshape(b * t, GLA_V), o_b.reshape(b * t, GLA_V), o_d.reshape(b * t, DIFF_V),
                    o_c.reshape(b * t, CROSS_Q), gates, gla_norm_g[l], w_branch[l].astype(BF16),
                    w_out[l].astype(BF16), final_norm_g, l == DEPTH - 1, 256)
    return x2d.reshape(b, t, d)


def kernel(x_prompt, x_sample, mem_prompt, mem_sample, norm_g, w_in, gla_gate_w2, gla_gate_b, gla_norm_g,
           diff_lambda, diff_norm_g, mem_norm_g, w_mem_kv, w_branch, w_out, final_norm_g):
    params = (norm_g, w_in, gla_gate_w2, gla_gate_b, gla_norm_g, diff_lambda, diff_norm_g, mem_norm_g,
              w_mem_kv, w_branch, w_out, final_norm_g)
    return (_trunk(x_prompt, mem_prompt, *params), _trunk(x_sample, mem_sample, *params))
```

```python
import functools
import math

import numpy as np
import jax
import jax.numpy as jnp
from jax import lax
from jax.experimental import pallas as pl
from jax.experimental.pallas import tpu as pltpu

F32 = jnp.float32
BF16 = jnp.bfloat16

D_MODEL = 1024
DEPTH = 4
N_MEM = 256
BRANCH_W = 512
N_BRANCH = 3
EPS = 1e-6
GLA_HEADS = 4
GLA_DK = 64
GLA_DV = 128
GLA_RANK = 16
GLA_TAU = 16.0
GLA_Q = GLA_HEADS * GLA_DK
GLA_V = GLA_HEADS * GLA_DV
DIFF_HEADS = 4
DIFF_HD = 64
DIFF_DV = 2 * DIFF_HD
DIFF_QK = DIFF_HEADS * 2 * DIFF_HD
DIFF_V = DIFF_HEADS * DIFF_DV
CROSS_HEADS = 4
CROSS_HD = 128
CROSS_Q = CROSS_HEADS * CROSS_HD
SPLITS = (GLA_Q, GLA_Q, GLA_V, 2 * GLA_RANK, BRANCH_W,
          DIFF_QK, DIFF_QK, DIFF_V, BRANCH_W,
          CROSS_Q, BRANCH_W, N_BRANCH * D_MODEL)

LANES = 128
VMEM_LIMIT = 56 * 1024 * 1024
GLA_CHUNK = 64
GLA_LEVELS = 6
GLR_PAD = LANES
GLA_IN = 2 * GLA_Q + GLA_V + GLR_PAD

_NT = (((1,), (1,)), ((), ()))
_TN = (((0,), (0,)), ((), ()))


def _cparams(sem):
    return pltpu.CompilerParams(dimension_semantics=sem, vmem_limit_bytes=VMEM_LIMIT)


def _norm_matmul_kernel(x_ref, g_ref, w_ref, o_ref, h_ref):
    @pl.when(pl.program_id(1) == 0)
    def _():
        x = x_ref[...]
        y = x * lax.rsqrt(jnp.mean(x * x, axis=-1, keepdims=True) + EPS)
        h_ref[...] = (y * g_ref[...]).astype(BF16)

    o_ref[...] = jnp.dot(h_ref[...], w_ref[...], preferred_element_type=F32).astype(o_ref.dtype)


def norm_matmul(x2d, g, w_bf16, out_dtype, tm, tn):
    m, d = x2d.shape
    n = w_bf16.shape[1]
    return pl.pallas_call(
        _norm_matmul_kernel,
        out_shape=jax.ShapeDtypeStruct((m, n), out_dtype),
        grid=(m // tm, n // tn),
        in_specs=[pl.BlockSpec((tm, d), lambda i, j: (i, 0)),
                  pl.BlockSpec((1, d), lambda i, j: (0, 0)),
                  pl.BlockSpec((d, tn), lambda i, j: (0, j))],
        out_specs=pl.BlockSpec((tm, tn), lambda i, j: (i, j)),
        scratch_shapes=[pltpu.VMEM((tm, d), BF16)],
        compiler_params=_cparams(("parallel", "arbitrary")),
        name="norm_matmul",
    )(x2d, g.reshape(1, d).astype(F32), w_bf16)


def _gla_constants(reverse):
    c = GLA_CHUNK
    t = np.arange(c)[:, None]
    u = np.arange(c)[None, :]
    blocks = [(u <= t).astype(np.float32), (u > t).astype(np.float32)]
    masks = [(t == u).astype(np.float32)]
    for lvl in range(1, GLA_LEVELS + 1):
        bs, half = 1 << lvl, 1 << (lvl - 1)
        start = (t // bs) * bs
        boundary = start + half - 1
        right = (t % bs) >= half
        m = np.where(right, (u > boundary) & (u <= t), (u > t) & (u <= boundary))
        blocks.append(m.astype(np.float32))
        s = np.arange(c)[None, :]
        same = (s // bs) == (t // bs)
        masks.append((same & right & ((s % bs) < half)).astype(np.float32))
    mall = np.concatenate(blocks, axis=0)
    mask = np.stack(masks, axis=0)
    if reverse:
        mall = mall.reshape(-1, c, c)[:, ::-1, ::-1].reshape(-1, c)
        mask = mask[:, ::-1, ::-1]
    mask = np.concatenate([mask, mask], axis=-1)
    return jnp.asarray(mall, BF16), jnp.asarray(np.ascontiguousarray(mask), F32)


def _gla_kernel(g_ref, mall_ref, mask_ref, w2_ref, b2_ref, o_ref, st_ref, *, reverse, n_chunks):
    c = GLA_CHUNK

    @pl.when(pl.program_id(1) == 0)
    def _():
        st_ref[...] = jnp.zeros_like(st_ref)

    lane = lax.broadcasted_iota(jnp.int32, (c, LANES), 1)
    lane_v = lax.broadcasted_iota(jnp.int32, (c, 2 * GLA_DV), 1)
    lo, hi = lane < GLA_DK, lane >= GLA_DK
    vlo, vhi = lane_v < GLA_DV, lane_v >= GLA_DV
    srow = lax.broadcasted_iota(jnp.int32, (2 * GLA_DV, LANES), 0)
    slane = lax.broadcasted_iota(jnp.int32, (2 * GLA_DV, LANES), 1)
    state_mask = (srow < GLA_DV) == (slane < GLA_DK)
    end_row = 0 if reverse else c - 1

    def chunk(ci, carry):
        idx = (n_chunks - 1 - ci) if reverse else ci
        rows = pl.ds(pl.multiple_of(idx * c, c), c)
        glr = g_ref[0, rows, 2 * GLA_Q + GLA_V:GLA_IN]
        logits = jnp.dot(glr.astype(BF16), w2_ref[...], preferred_element_type=F32) + b2_ref[...]
        log_a = (jnp.minimum(logits, 0.0) - jnp.log1p(jnp.exp(-jnp.abs(logits)))) * (1.0 / GLA_TAU)
        e_all = jnp.exp(jnp.dot(mall_ref[...], log_a.astype(BF16), preferred_element_type=F32))
        for p in range(GLA_HEADS // 2):
            lanes = slice(p * LANES, (p + 1) * LANES)
            q = g_ref[0, rows, p * LANES:(p + 1) * LANES] * (GLA_DK ** -0.5)
            k = g_ref[0, rows, GLA_Q + p * LANES:GLA_Q + (p + 1) * LANES]
            v = g_ref[0, rows, 2 * GLA_Q + p * 2 * GLA_DV:2 * GLA_Q + (p + 1) * 2 * GLA_DV]
            v2 = jnp.concatenate([jnp.where(vlo, v, 0.0), jnp.where(vhi, v, 0.0)], axis=0).astype(BF16)
            e_start = e_all[0:c, lanes]
            e_end = e_all[c:2 * c, lanes]
            st = st_ref[p]
            o = lax.dot_general((q * e_start).astype(BF16), st.astype(BF16), _NT,
                                preferred_element_type=F32)
            att = jnp.zeros((c, LANES), F32)
            for lvl in range(GLA_LEVELS + 1):
                if lvl == 0:
                    qt, kt = q, k
                else:
                    e = e_all[(lvl + 1) * c:(lvl + 2) * c, lanes]
                    qt, kt = q * e, k * e
                k2 = jnp.concatenate([jnp.where(lo, kt, 0.0), jnp.where(hi, kt, 0.0)], axis=0)
                a = lax.dot_general(qt.astype(BF16), k2.astype(BF16), _NT, preferred_element_type=F32)
                att = att + a * mask_ref[lvl]
            o = o + jnp.dot(att.astype(BF16), v2, preferred_element_type=F32)
            o_ref[0, rows, p * 2 * GLA_DV:(p + 1) * 2 * GLA_DV] = o
            upd = lax.dot_general(v.astype(BF16), (k * e_end).astype(BF16), _TN,
                                  preferred_element_type=F32)
            decay = e_start[end_row:end_row + 1, :]
            st_ref[p] = st * decay + jnp.where(state_mask, upd, 0.0)
        return carry

    lax.fori_loop(0, n_chunks, chunk, 0)


def gla(g_in, w2pad, b2, reverse, tb):
    b, t, _ = g_in.shape
    nblk = t // tb
    mall, mask = _gla_constants(reverse)
    tmap = (lambda bi, i: (bi, nblk - 1 - i, 0)) if reverse else (lambda bi, i: (bi, i, 0))
    const2 = lambda bi, i: (0, 0)
    return pl.pallas_call(
        functools.partial(_gla_kernel, reverse=reverse, n_chunks=tb // GLA_CHUNK),
        out_shape=jax.ShapeDtypeStruct((b, t, GLA_V), F32),
        grid=(b, nblk),
        in_specs=[pl.BlockSpec((1, tb, GLA_IN), tmap),
                  pl.BlockSpec(mall.shape, const2),
                  pl.BlockSpec(mask.shape, lambda bi, i: (0, 0, 0)),
                  pl.BlockSpec(w2pad.shape, const2),
                  pl.BlockSpec(b2.shape, const2)],
        out_specs=pl.BlockSpec((1, tb, GLA_V), tmap),
        scratch_shapes=[pltpu.VMEM((GLA_HEADS // 2, 2 * GLA_DV, LANES), F32)],
        compiler_params=_cparams(("parallel", "arbitrary")),
        name="gla_bwd" if reverse else "gla_fwd",
    )(g_in, mall, mask, w2pad, b2)


def _alibi_constants(t, tq):
    slopes = 2.0 ** (-8.0 * (np.arange(DIFF_HEADS, dtype=np.float64) + 1.0) / DIFF_HEADS)
    pos = np.arange(t)
    hi_part = (pos // LANES) * LANES
    lo_part = pos % LANES
    kaug = np.zeros((DIFF_HEADS, t, LANES), np.float32)
    qaug = np.zeros((DIFF_HEADS, t, LANES), np.float32)
    for h in range(DIFF_HEADS):
        kaug[h, :, 0] = -slopes[h] * hi_part
        kaug[h, :, 1] = -slopes[h] * lo_part
        kaug[h, :, 2] = 1.0
        kaug[h, :, 3] = 1.0
        qaug[h, :, 0] = 1.0
        qaug[h, :, 1] = 1.0
        qaug[h, :, 2] = slopes[h] * hi_part
        qaug[h, :, 3] = slopes[h] * lo_part
    i = np.arange(tq)
    dtab = -slopes[:, None, None] * np.abs(i[:, None] - i[None, :])[None]
    dtab = np.concatenate([dtab, dtab], axis=1)
    return jnp.asarray(kaug, BF16), jnp.asarray(qaug, BF16), jnp.asarray(dtab, F32)


def _diff_kernel(q_ref, k_ref, v_ref, kaug_ref, qaug_ref, dtab_ref, lp_ref, gn_ref, o_ref,
                 kp_ref, vp_ref, lhs_ref, acc_ref, s_ref, *, tq, nk, lambda_init):
    qi = pl.program_id(2)

    @pl.when(qi == 0)
    def _():
        kp_ref[:, 0:LANES] = k_ref[0]
        kp_ref[:, LANES:2 * LANES] = kaug_ref[0]
        vp_ref[:, 0:LANES] = v_ref[0]
        vp_ref[:, LANES:2 * LANES] = jnp.ones((vp_ref.shape[0], LANES), BF16)

    lane = lax.broadcasted_iota(jnp.int32, (tq, LANES), 1)
    qs = q_ref[0] * (DIFF_HD ** -0.5)
    qq = jnp.concatenate([jnp.where(lane < DIFF_HD, qs, 0.0), jnp.where(lane >= DIFF_HD, qs, 0.0)], axis=0)
    qa = jnp.concatenate([qaug_ref[0], qaug_ref[0]], axis=0)
    lhs_ref[:, :, 0:LANES] = jnp.broadcast_to(qq.astype(BF16)[None], (3, 2 * tq, LANES))
    lhs_ref[0, :, LANES:2 * LANES] = jnp.zeros((2 * tq, LANES), BF16)
    lhs_ref[1, :, LANES:2 * LANES] = -qa
    lhs_ref[2, :, LANES:2 * LANES] = qa
    acc_ref[...] = jnp.zeros_like(acc_ref)

    def key_rows(kj):
        return pl.ds(pl.multiple_of(kj * tq, tq), tq)

    def scores(kj, side):
        return lax.dot_general(lhs_ref[side], kp_ref[key_rows(kj), :], _NT, preferred_element_type=F32)

    def accumulate(s, kj, m):
        m_new = jnp.maximum(m, jnp.max(s, axis=-1, keepdims=True))
        p = jnp.exp(s - m_new)
        acc_ref[...] = acc_ref[...] * jnp.exp(m - m_new) + jnp.dot(p.astype(BF16), vp_ref[key_rows(kj), :],
                                                                  preferred_element_type=F32)
        return m_new

    def block(n):
        return jnp.where(n == 0, qi, jnp.where(n - 1 < qi, n - 1, n))

    def other_scores(n):
        kj = block(n)
        return scores(kj, jnp.where(kj < qi, 1, 2))

    assert nk % 2 == 0
    s_ref[0] = scores(qi, 0) + dtab_ref[0]

    def pair(t, m):
        s_ref[1] = other_scores(2 * t + 1)
        m = accumulate(s_ref[0], block(2 * t), m)
        s_ref[0] = other_scores(2 * t + 2)
        return accumulate(s_ref[1], block(2 * t + 1), m)

    m = lax.fori_loop(0, nk // 2 - 1, pair, jnp.full((2 * tq, 1), -jnp.inf, F32))
    s_ref[1] = other_scores(nk - 1)
    m = accumulate(s_ref[0], block(nk - 2), m)
    accumulate(s_ref[1], block(nk - 1), m)

    lp = lp_ref[...]
    lam = (jnp.exp(jnp.sum(lp[0:1] * lp[1:2], axis=-1, keepdims=True))
           - jnp.exp(jnp.sum(lp[2:3] * lp[3:4], axis=-1, keepdims=True)) + lambda_init)
    a1, a2 = acc_ref[0:tq], acc_ref[tq:2 * tq]
    o = a1[:, 0:LANES] / a1[:, LANES:2 * LANES] - lam * (a2[:, 0:LANES] / a2[:, LANES:2 * LANES])
    y = o * lax.rsqrt(jnp.mean(o * o, axis=-1, keepdims=True) + EPS)
    o_ref[0] = (y * gn_ref[...]) * (1.0 - lambda_init)


def diff_attn(qkv, lam_params, g_norm, lambda_init, tq):
    b, t, _ = qkv.shape
    nq = t // tq
    kaug, qaug, dtab = _alibi_constants(t, tq)
    h_ = DIFF_HEADS
    return pl.pallas_call(
        functools.partial(_diff_kernel, tq=tq, nk=nq, lambda_init=lambda_init),
        out_shape=jax.ShapeDtypeStruct((b, t, DIFF_V), F32),
        grid=(b, h_, nq),
        in_specs=[pl.BlockSpec((1, tq, LANES), lambda bi, h, qi: (bi, qi, h)),
                  pl.BlockSpec((1, t, LANES), lambda bi, h, qi: (bi, 0, h_ + h)),
                  pl.BlockSpec((1, t, LANES), lambda bi, h, qi: (bi, 0, 2 * h_ + h)),
                  pl.BlockSpec((1, t, LANES), lambda bi, h, qi: (h, 0, 0)),
                  pl.BlockSpec((1, tq, LANES), lambda bi, h, qi: (h, qi, 0)),
                  pl.BlockSpec((1, 2 * tq, tq), lambda bi, h, qi: (h, 0, 0)),
                  pl.BlockSpec((4, DIFF_HD), lambda bi, h, qi: (0, 0)),
                  pl.BlockSpec((1, DIFF_DV), lambda bi, h, qi: (0, 0))],
        out_specs=pl.BlockSpec((1, tq, LANES), lambda bi, h, qi: (bi, qi, h)),
        scratch_shapes=[pltpu.VMEM((t, 2 * LANES), BF16),
                        pltpu.VMEM((t, 2 * LANES), BF16),
                        pltpu.VMEM((3, 2 * tq, 2 * LANES), BF16),
                        pltpu.VMEM((2 * tq, 2 * LANES), F32),
                        pltpu.VMEM((2, 2 * tq, tq), F32)],
        compiler_params=_cparams(("parallel", "parallel", "arbitrary")),
        name="diff_attn",
    )(qkv, qkv, qkv, kaug, qaug, dtab, lam_params.astype(F32), g_norm.reshape(1, DIFF_DV).astype(F32))


def _cross_kernel(q_ref, kv_ref, o_ref):
    for h in range(CROSS_HEADS):
        qh = q_ref[0, :, h * CROSS_HD:(h + 1) * CROSS_HD]
        kh = kv_ref[0, :, h * CROSS_HD:(h + 1) * CROSS_HD]
        vh = kv_ref[0, :, CROSS_Q + h * CROSS_HD:CROSS_Q + (h + 1) * CROSS_HD]
        s = lax.dot_general(qh, kh, _NT, preferred_element_type=F32) * (CROSS_HD ** -0.5)
        p = jnp.exp(s - jnp.max(s, axis=-1, keepdims=True))
        l = jnp.sum(p, axis=-1, keepdims=True)
        o = jnp.dot(p.astype(BF16), vh, preferred_element_type=F32)
        o_ref[0, :, h * CROSS_HD:(h + 1) * CROSS_HD] = o / l


def cross_attn(q, kv, tq):
    b, t, _ = q.shape
    return pl.pallas_call(
        _cross_kernel,
        out_shape=jax.ShapeDtypeStruct((b, t, CROSS_Q), F32),
        grid=(b, t // tq),
        in_specs=[pl.BlockSpec((1, tq, CROSS_Q), lambda bi, i: (bi, i, 0)),
                  pl.BlockSpec((1, N_MEM, 2 * CROSS_Q), lambda bi, i: (bi, 0, 0))],
        out_specs=pl.BlockSpec((1, tq, CROSS_Q), lambda bi, i: (bi, i, 0)),
        compiler_params=_cparams(("parallel", "parallel")),
        name="cross_attn",
    )(q, kv)


def _silu(z):
    return z / (1.0 + jnp.exp(-z))


def _sigmoid(z):
    return 1.0 / (1.0 + jnp.exp(-z))


def _merge_kernel(x_ref, of_ref, ob_ref, od_ref, oc_ref, gt_ref, gn_ref, wb_ref, wo_ref, fg_ref, o_ref,
                  *, final):
    og = of_ref[...] + ob_ref[...]
    parts = []
    for h in range(GLA_HEADS):
        oh = og[:, h * GLA_DV:(h + 1) * GLA_DV]
        parts.append(oh * lax.rsqrt(jnp.mean(oh * oh, axis=-1, keepdims=True) + EPS) * gn_ref[...])
    branches = (jnp.concatenate(parts, axis=1), od_ref[...], oc_ref[...])
    merged = None
    for i in range(N_BRANCH):
        z = gt_ref[:, i * BRANCH_W:(i + 1) * BRANCH_W]
        a = (branches[i] * _silu(z)).astype(BF16)
        y = jnp.dot(a, wb_ref[i], preferred_element_type=F32)
        gate = _sigmoid(gt_ref[:, N_BRANCH * BRANCH_W + i * D_MODEL:N_BRANCH * BRANCH_W + (i + 1) * D_MODEL])
        merged = gate * y if merged is None else merged + gate * y
    x = x_ref[...] + jnp.dot(merged.astype(BF16), wo_ref[...], preferred_element_type=F32)
    if final:
        x = (x * lax.rsqrt(jnp.mean(x * x, axis=-1, keepdims=True) + EPS)) * fg_ref[...]
    o_ref[...] = x


def merge(x2d, of, ob, od, oc, gates, gla_norm_g, wb, wo, final_g, final, tm):
    m, d = x2d.shape
    ng = gates.shape[1]
    row = lambda w: pl.BlockSpec((tm, w), lambda i: (i, 0))
    return pl.pallas_call(
        functools.partial(_merge_kernel, final=final),
        out_shape=jax.ShapeDtypeStruct((m, d), F32),
        grid=(m // tm,),
        in_specs=[row(d), row(BRANCH_W), row(BRANCH_W), row(BRANCH_W), row(BRANCH_W), row(ng),
                  pl.BlockSpec((1, GLA_DV), lambda i: (0, 0)),
                  pl.BlockSpec((N_BRANCH, BRANCH_W, d), lambda i: (0, 0, 0)),
                  pl.BlockSpec((d, d), lambda i: (0, 0)),
                  pl.BlockSpec((1, d), lambda i: (0, 0))],
        out_specs=row(d),
        compiler_params=_cparams(("parallel",)),
        name="merge_final" if final else "merge",
    )(x2d, of, ob, od, oc, gates, gla_norm_g.reshape(1, GLA_DV).astype(F32), wb, wo,
      final_g.reshape(1, d).astype(F32))


def _split_w_in(w_in_l):
    offs = np.concatenate([[0], np.cumsum(np.array(SPLITS))])
    col = lambda i: w_in_l[:, int(offs[i]):int(offs[i + 1])]
    g_q, g_k, g_v, g_lr, z_a, d_q, d_k, d_v, z_b, c_q, z_c, gate_in = (col(i) for i in range(len(SPLITS)))
    lr_pad = jnp.pad(g_lr, ((0, 0), (0, GLR_PAD - 2 * GLA_RANK)))
    w_gla = jnp.concatenate([g_q, g_k, g_v, lr_pad], axis=1).astype(BF16)
    w_diff = jnp.concatenate([d_q, d_k, d_v], axis=1).astype(BF16)
    w_gates = jnp.concatenate([z_a, z_b, z_c, gate_in], axis=1).astype(BF16)
    return w_gla, w_diff, c_q.astype(BF16), w_gates


def _gate_weights(w2_l, b2_l, direction):
    w2pad = jnp.zeros((GLR_PAD, GLA_Q), F32)
    w2pad = w2pad.at[direction * GLA_RANK:(direction + 1) * GLA_RANK].set(w2_l[direction].astype(F32))
    return w2pad.astype(BF16), b2_l[direction].reshape(1, GLA_Q).astype(F32)


def _tiles(t):
    return dict(tm=512, tb=min(t, 512), tq=512, tc=min(t, 512))


def _trunk(x, mem, norm_g, w_in, gla_gate_w2, gla_gate_b, gla_norm_g, diff_lambda, diff_norm_g,
           mem_norm_g, w_mem_kv, w_branch, w_out, final_norm_g):
    b, t, d = x.shape
    tl = _tiles(t)
    x2d = x.reshape(b * t, d)
    mem2d = mem.reshape(b * N_MEM, d)
    for l in range(DEPTH):
        lambda_init = 0.8 - 0.6 * math.exp(-0.3 * l)
        w_gla, w_diff, w_cq, w_gates = _split_w_in(w_in[l])
        g_in = norm_matmul(x2d, norm_g[l], w_gla, F32, tl["tm"], GLA_IN).reshape(b, t, GLA_IN)
        qkv = norm_matmul(x2d, norm_g[l], w_diff, BF16, tl["tm"], 3 * DIFF_QK).reshape(b, t, 3 * DIFF_QK)
        cq = norm_matmul(x2d, norm_g[l], w_cq, BF16, tl["tm"], CROSS_Q).reshape(b, t, CROSS_Q)
        gates = norm_matmul(x2d, norm_g[l], w_gates, F32, tl["tm"], 1536)
        kv = norm_matmul(mem2d, mem_norm_g[l], w_mem_kv[l].astype(BF16), BF16, N_MEM, 2 * CROSS_Q)
        kv = kv.reshape(b, N_MEM, 2 * CROSS_Q)
        o_f = gla(g_in, *_gate_weights(gla_gate_w2[l], gla_gate_b[l], 0), False, tl["tb"])
        o_b = gla(g_in, *_gate_weights(gla_gate_w2[l], gla_gate_b[l], 1), True, tl["tb"])
        o_d = diff_attn(qkv, diff_lambda[l], diff_norm_g[l], lambda_init, tl["tq"])
        o_c = cross_attn(cq, kv, tl["tc"])
        x2d = merge(x2d, o_f.reshape(b * t, GLA_V), o_b.reshape(b * t, GLA_V), o_d.reshape(b * t, DIFF_V),
                    o_c.reshape(b * t, CROSS_Q), gates, gla_norm_g[l], w_branch[l].astype(BF16),
                    w_out[l].astype(BF16), final_norm_g, l == DEPTH - 1, 256)
    return x2d.reshape(b, t, d)


def kernel(x_prompt, x_sample, mem_prompt, mem_sample, norm_g, w_in, gla_gate_w2, gla_gate_b, gla_norm_g,
           diff_lambda, diff_norm_g, mem_norm_g, w_mem_kv, w_branch, w_out, final_norm_g):
    params = (norm_g, w_in, gla_gate_w2, gla_gate_b, gla_norm_g, diff_lambda, diff_norm_g, mem_norm_g,
              w_mem_kv, w_branch, w_out, final_norm_g)
    return (_trunk(x_prompt, mem_prompt, *params), _trunk(x_sample, mem_sample, *params))
```

```python
import functools
import math

import numpy as np
import jax
import jax.numpy as jnp
from jax import lax
from jax.experimental import pallas as pl
from jax.experimental.pallas import tpu as pltpu

F32 = jnp.float32
BF16 = jnp.bfloat16

D_MODEL = 1024
DEPTH = 4
N_MEM = 256
BRANCH_W = 512
N_BRANCH = 3
EPS = 1e-6
GLA_HEADS = 4
GLA_DK = 64
GLA_DV = 128
GLA_RANK = 16
GLA_TAU = 16.0
GLA_Q = GLA_HEADS * GLA_DK
GLA_V = GLA_HEADS * GLA_DV
DIFF_HEADS = 4
DIFF_HD = 64
DIFF_DV = 2 * DIFF_HD
DIFF_QK = DIFF_HEADS * 2 * DIFF_HD
DIFF_V = DIFF_HEADS * DIFF_DV
CROSS_HEADS = 4
CROSS_HD = 128
CROSS_Q = CROSS_HEADS * CROSS_HD
SPLITS = (GLA_Q, GLA_Q, GLA_V, 2 * GLA_RANK, BRANCH_W,
          DIFF_QK, DIFF_QK, DIFF_V, BRANCH_W,
          CROSS_Q, BRANCH_W, N_BRANCH * D_MODEL)

LANES = 128
VMEM_LIMIT = 56 * 1024 * 1024
GLA_CHUNK = 128
GLA_LEVELS = 7
GLR_PAD = LANES
GLA_IN = 2 * GLA_Q + GLA_V + GLR_PAD

_NT = (((1,), (1,)), ((), ()))
_TN = (((0,), (0,)), ((), ()))


def _cparams(sem):
    return pltpu.CompilerParams(dimension_semantics=sem, vmem_limit_bytes=VMEM_LIMIT)


def _resident(shape):
    return pl.BlockSpec(shape, lambda *_: (0,) * len(shape), pipeline_mode=pl.Buffered(1))


def _norm_matmul_kernel(x_ref, g_ref, w_ref, o_ref, h_ref):
    @pl.when(pl.program_id(1) == 0)
    def _():
        x = x_ref[...]
        y = x * lax.rsqrt(jnp.mean(x * x, axis=-1, keepdims=True) + EPS)
        h_ref[...] = (y * g_ref[...]).astype(BF16)

    o_ref[...] = jnp.dot(h_ref[...], w_ref[...], preferred_element_type=F32).astype(o_ref.dtype)


def norm_matmul(x2d, g, w_bf16, out_dtype, tm, tn):
    m, d = x2d.shape
    n = w_bf16.shape[1]
    return pl.pallas_call(
        _norm_matmul_kernel,
        out_shape=jax.ShapeDtypeStruct((m, n), out_dtype),
        grid=(m // tm, n // tn),
        in_specs=[pl.BlockSpec((tm, d), lambda i, j: (i, 0)),
                  pl.BlockSpec((1, d), lambda i, j: (0, 0)),
                  pl.BlockSpec((d, tn), lambda i, j: (0, j))],
        out_specs=pl.BlockSpec((tm, tn), lambda i, j: (i, j)),
        scratch_shapes=[pltpu.VMEM((tm, d), BF16)],
        compiler_params=_cparams(("parallel", "arbitrary")),
        name="norm_matmul",
    )(x2d, g.reshape(1, d).astype(F32), w_bf16)


def _proj_kernel(x_ref, g_ref, w_ref, gla_ref, qkv_ref, cq_ref):
    x = x_ref[...]
    h = ((x * lax.rsqrt(jnp.mean(x * x, axis=-1, keepdims=True) + EPS)) * g_ref[...]).astype(BF16)
    c1, c2 = GLA_IN, GLA_IN + 3 * DIFF_QK
    gla_ref[...] = jnp.dot(h, w_ref[:, 0:c1], preferred_element_type=F32)
    qkv_ref[...] = jnp.dot(h, w_ref[:, c1:c2], preferred_element_type=F32).astype(BF16)
    cq_ref[...] = jnp.dot(h, w_ref[:, c2:c2 + CROSS_Q], preferred_element_type=F32).astype(BF16)


def branch_inputs(x2d, g, w_bf16, tm):
    m, d = x2d.shape
    row = lambda w: pl.BlockSpec((tm, w), lambda i: (i, 0))
    return pl.pallas_call(
        _proj_kernel,
        out_shape=(jax.ShapeDtypeStruct((m, GLA_IN), F32),
                   jax.ShapeDtypeStruct((m, 3 * DIFF_QK), BF16),
                   jax.ShapeDtypeStruct((m, CROSS_Q), BF16)),
        grid=(m // tm,),
        in_specs=[row(d), _resident((1, d)), _resident(w_bf16.shape)],
        out_specs=(row(GLA_IN), row(3 * DIFF_QK), row(CROSS_Q)),
        compiler_params=_cparams(("parallel",)),
        name="branch_inputs",
    )(x2d, g.reshape(1, d).astype(F32), w_bf16)


def _gla_constants(reverse):
    c = GLA_CHUNK
    t = np.arange(c)[:, None]
    u = np.arange(c)[None, :]
    blocks = [(u <= t).astype(np.float32), (u > t).astype(np.float32)]
    masks = [(t == u).astype(np.float32)]
    for lvl in range(1, GLA_LEVELS + 1):
        bs, half = 1 << lvl, 1 << (lvl - 1)
        start = (t // bs) * bs
        boundary = start + half - 1
        right = (t % bs) >= half
        m = np.where(right, (u > boundary) & (u <= t), (u > t) & (u <= boundary))
        blocks.append(m.astype(np.float32))
        s = np.arange(c)[None, :]
        same = (s // bs) == (t // bs)
        masks.append((same & right & ((s % bs) < half)).astype(np.float32))
    mall = np.concatenate(blocks, axis=0)
    mask = np.stack(masks, axis=0)
    if reverse:
        mall = mall.reshape(-1, c, c)[:, ::-1, ::-1].reshape(-1, c)
        mask = mask[:, ::-1, ::-1]
    mask = np.concatenate([mask, mask], axis=-1)
    return jnp.asarray(mall, BF16), jnp.asarray(np.ascontiguousarray(mask), F32)


def _gla_kernel(g_ref, mall_ref, mask_ref, w2_ref, b2_ref, o_ref, st_ref, *, reverse, n_chunks):
    c = GLA_CHUNK

    @pl.when(pl.program_id(1) == 0)
    def _():
        st_ref[...] = jnp.zeros_like(st_ref)

    lane = lax.broadcasted_iota(jnp.int32, (c, LANES), 1)
    lane_v = lax.broadcasted_iota(jnp.int32, (c, 2 * GLA_DV), 1)
    lo, hi = lane < GLA_DK, lane >= GLA_DK
    vlo, vhi = lane_v < GLA_DV, lane_v >= GLA_DV
    srow = lax.broadcasted_iota(jnp.int32, (2 * GLA_DV, LANES), 0)
    slane = lax.broadcasted_iota(jnp.int32, (2 * GLA_DV, LANES), 1)
    state_mask = (srow < GLA_DV) == (slane < GLA_DK)
    end_row = 0 if reverse else c - 1

    def chunk(ci, carry):
        idx = (n_chunks - 1 - ci) if reverse else ci
        rows = pl.ds(pl.multiple_of(idx * c, c), c)
        glr = g_ref[0, rows, 2 * GLA_Q + GLA_V:GLA_IN]
        logits = jnp.dot(glr.astype(BF16), w2_ref[...], preferred_element_type=F32) + b2_ref[...]
        log_a = (jnp.minimum(logits, 0.0) - jnp.log1p(jnp.exp(-jnp.abs(logits)))) * (1.0 / GLA_TAU)
        e_all = jnp.exp(jnp.dot(mall_ref[...], log_a.astype(BF16), preferred_element_type=F32))
        for p in range(GLA_HEADS // 2):
            lanes = slice(p * LANES, (p + 1) * LANES)
            q = g_ref[0, rows, p * LANES:(p + 1) * LANES] * (GLA_DK ** -0.5)
            k = g_ref[0, rows, GLA_Q + p * LANES:GLA_Q + (p + 1) * LANES]
            v = g_ref[0, rows, 2 * GLA_Q + p * 2 * GLA_DV:2 * GLA_Q + (p + 1) * 2 * GLA_DV]
            v2 = jnp.concatenate([jnp.where(vlo, v, 0.0), jnp.where(vhi, v, 0.0)], axis=0).astype(BF16)
            e_start = e_all[0:c, lanes]
            e_end = e_all[c:2 * c, lanes]
            st = st_ref[p]
            o = lax.dot_general((q * e_start).astype(BF16), st.astype(BF16), _NT,
                                preferred_element_type=F32)
            att = jnp.zeros((c, 2 * c), F32)
            for lvl in range(GLA_LEVELS + 1):
                if lvl == 0:
                    qt, kt = q, k
                else:
                    e = e_all[(lvl + 1) * c:(lvl + 2) * c, lanes]
                    qt, kt = q * e, k * e
                k2 = jnp.concatenate([jnp.where(lo, kt, 0.0), jnp.where(hi, kt, 0.0)], axis=0)
                a = lax.dot_general(qt.astype(BF16), k2.astype(BF16), _NT, preferred_element_type=F32)
                att = att + a * mask_ref[lvl]
            o = o + jnp.dot(att.astype(BF16), v2, preferred_element_type=F32)
            o_ref[0, rows, p * 2 * GLA_DV:(p + 1) * 2 * GLA_DV] = o
            upd = lax.dot_general(v.astype(BF16), (k * e_end).astype(BF16), _TN,
                                  preferred_element_type=F32)
            decay = e_start[end_row:end_row + 1, :]
            st_ref[p] = st * decay + jnp.where(state_mask, upd, 0.0)
        return carry

    lax.fori_loop(0, n_chunks, chunk, 0, unroll=4)


def gla(g_in, w2pad, b2, reverse, tb):
    b, t, _ = g_in.shape
    nblk = t // tb
    mall, mask = _gla_constants(reverse)
    tmap = (lambda bi, i: (bi, nblk - 1 - i, 0)) if reverse else (lambda bi, i: (bi, i, 0))
    const2 = lambda bi, i: (0, 0)
    return pl.pallas_call(
        functools.partial(_gla_kernel, reverse=reverse, n_chunks=tb // GLA_CHUNK),
        out_shape=jax.ShapeDtypeStruct((b, t, GLA_V), F32),
        grid=(b, nblk),
        in_specs=[pl.BlockSpec((1, tb, GLA_IN), tmap),
                  pl.BlockSpec(mall.shape, const2),
                  pl.BlockSpec(mask.shape, lambda bi, i: (0, 0, 0)),
                  pl.BlockSpec(w2pad.shape, const2),
                  pl.BlockSpec(b2.shape, const2)],
        out_specs=pl.BlockSpec((1, tb, GLA_V), tmap),
        scratch_shapes=[pltpu.VMEM((GLA_HEADS // 2, 2 * GLA_DV, LANES), F32)],
        compiler_params=_cparams(("parallel", "arbitrary")),
        name="gla_bwd" if reverse else "gla_fwd",
    )(g_in, mall, mask, w2pad, b2)


def _alibi_constants(t, tq):
    slopes = 2.0 ** (-8.0 * (np.arange(DIFF_HEADS, dtype=np.float64) + 1.0) / DIFF_HEADS)
    pos = np.arange(t)
    hi_part = (pos // LANES) * LANES
    lo_part = pos % LANES
    kaug = np.zeros((DIFF_HEADS, t, LANES), np.float32)
    qaug = np.zeros((DIFF_HEADS, t, LANES), np.float32)
    for h in range(DIFF_HEADS):
        kaug[h, :, 0] = -slopes[h] * hi_part
        kaug[h, :, 1] = -slopes[h] * lo_part
        kaug[h, :, 2] = 1.0
        kaug[h, :, 3] = 1.0
        qaug[h, :, 0] = 1.0
        qaug[h, :, 1] = 1.0
        qaug[h, :, 2] = slopes[h] * hi_part
        qaug[h, :, 3] = slopes[h] * lo_part
    i = np.arange(tq)
    dtab = -slopes[:, None, None] * np.abs(i[:, None] - i[None, :])[None]
    dtab = np.concatenate([dtab, dtab], axis=1)
    return jnp.asarray(kaug, BF16), jnp.asarray(qaug, BF16), jnp.asarray(dtab, F32)


def _diff_kernel(q_ref, k_ref, v_ref, kaug_ref, qaug_ref, dtab_ref, lp_ref, gn_ref, o_ref,
                 kp_ref, vp_ref, lhs_ref, acc_ref, s_ref, *, tq, nk, lambda_init):
    qi = pl.program_id(2)

    @pl.when(qi == 0)
    def _():
        kp_ref[:, 0:LANES] = k_ref[0]
        kp_ref[:, LANES:2 * LANES] = kaug_ref[0]
        vp_ref[:, 0:LANES] = v_ref[0]
        vp_ref[:, LANES:2 * LANES] = jnp.ones((vp_ref.shape[0], LANES), BF16)

    lane = lax.broadcasted_iota(jnp.int32, (tq, LANES), 1)
    qs = q_ref[0] * (DIFF_HD ** -0.5)
    qq = jnp.concatenate([jnp.where(lane < DIFF_HD, qs, 0.0), jnp.where(lane >= DIFF_HD, qs, 0.0)], axis=0)
    qa = jnp.concatenate([qaug_ref[0], qaug_ref[0]], axis=0)
    lhs_ref[:, :, 0:LANES] = jnp.broadcast_to(qq.astype(BF16)[None], (3, 2 * tq, LANES))
    lhs_ref[0, :, LANES:2 * LANES] = jnp.zeros((2 * tq, LANES), BF16)
    lhs_ref[1, :, LANES:2 * LANES] = -qa
    lhs_ref[2, :, LANES:2 * LANES] = qa
    acc_ref[...] = jnp.zeros_like(acc_ref)

    def key_rows(kj):
        return pl.ds(pl.multiple_of(kj * tq, tq), tq)

    def scores(kj, side):
        return lax.dot_general(lhs_ref[side], kp_ref[key_rows(kj), :], _NT, preferred_element_type=F32)

    def accumulate(s, kj, m):
        m_new = jnp.maximum(m, jnp.max(s, axis=-1, keepdims=True))
        p = jnp.exp(s - m_new)
        acc_ref[...] = acc_ref[...] * jnp.exp(m - m_new) + jnp.dot(p.astype(BF16), vp_ref[key_rows(kj), :],
                                                                  preferred_element_type=F32)
        return m_new

    def block(n):
        return jnp.where(n == 0, qi, jnp.where(n - 1 < qi, n - 1, n))

    def other_scores(n):
        kj = block(n)
        return scores(kj, jnp.where(kj < qi, 1, 2))

    assert nk % 2 == 0
    s_ref[0] = scores(qi, 0) + dtab_ref[0]

    def pair(t, m):
        s_ref[1] = other_scores(2 * t + 1)
        m = accumulate(s_ref[0], block(2 * t), m)
        s_ref[0] = other_scores(2 * t + 2)
        return accumulate(s_ref[1], block(2 * t + 1), m)

    m = lax.fori_loop(0, nk // 2 - 1, pair, jnp.full((2 * tq, 1), -jnp.inf, F32))
    s_ref[1] = other_scores(nk - 1)
    m = accumulate(s_ref[0], block(nk - 2), m)
    accumulate(s_ref[1], block(nk - 1), m)

    lp = lp_ref[...]
    lam = (jnp.exp(jnp.sum(lp[0:1] * lp[1:2], axis=-1, keepdims=True))
           - jnp.exp(jnp.sum(lp[2:3] * lp[3:4], axis=-1, keepdims=True)) + lambda_init)
    a1, a2 = acc_ref[0:tq], acc_ref[tq:2 * tq]
    o = a1[:, 0:LANES] / a1[:, LANES:2 * LANES] - lam * (a2[:, 0:LANES] / a2[:, LANES:2 * LANES])
    y = o * lax.rsqrt(jnp.mean(o * o, axis=-1, keepdims=True) + EPS)
    o_ref[0] = (y * gn_ref[...]) * (1.0 - lambda_init)


def diff_attn(qkv, lam_params, g_norm, lambda_init, tq):
    b, t, _ = qkv.shape
    nq = t // tq
    kaug, qaug, dtab = _alibi_constants(t, tq)
    h_ = DIFF_HEADS
    return pl.pallas_call(
        functools.partial(_diff_kernel, tq=tq, nk=nq, lambda_init=lambda_init),
        out_shape=jax.ShapeDtypeStruct((b, t, DIFF_V), F32),
        grid=(b, h_, nq),
        in_specs=[pl.BlockSpec((1, tq, LANES), lambda bi, h, qi: (bi, qi, h)),
                  pl.BlockSpec((1, t, LANES), lambda bi, h, qi: (bi, 0, h_ + h)),
                  pl.BlockSpec((1, t, LANES), lambda bi, h, qi: (bi, 0, 2 * h_ + h)),
                  pl.BlockSpec((1, t, LANES), lambda bi, h, qi: (h, 0, 0)),
                  pl.BlockSpec((1, tq, LANES), lambda bi, h, qi: (h, qi, 0)),
                  pl.BlockSpec((1, 2 * tq, tq), lambda bi, h, qi: (h, 0, 0)),
                  pl.BlockSpec((4, DIFF_HD), lambda bi, h, qi: (0, 0)),
                  pl.BlockSpec((1, DIFF_DV), lambda bi, h, qi: (0, 0))],
        out_specs=pl.BlockSpec((1, tq, LANES), lambda bi, h, qi: (bi, qi, h)),
        scratch_shapes=[pltpu.VMEM((t, 2 * LANES), BF16),
                        pltpu.VMEM((t, 2 * LANES), BF16),
                        pltpu.VMEM((3, 2 * tq, 2 * LANES), BF16),
                        pltpu.VMEM((2 * tq, 2 * LANES), F32),
                        pltpu.VMEM((2, 2 * tq, tq), F32)],
        compiler_params=_cparams(("parallel", "parallel", "arbitrary")),
        name="diff_attn",
    )(qkv, qkv, qkv, kaug, qaug, dtab, lam_params.astype(F32), g_norm.reshape(1, DIFF_DV).astype(F32))


def _cross_kernel(q_ref, kv_ref, o_ref):
    for h in range(CROSS_HEADS):
        qh = q_ref[0, :, h * CROSS_HD:(h + 1) * CROSS_HD]
        kh = kv_ref[0, :, h * CROSS_HD:(h + 1) * CROSS_HD]
        vh = kv_ref[0, :, CROSS_Q + h * CROSS_HD:CROSS_Q + (h + 1) * CROSS_HD]
        s = lax.dot_general(qh, kh, _NT, preferred_element_type=F32) * (CROSS_HD ** -0.5)
        p = jnp.exp(s - jnp.max(s, axis=-1, keepdims=True))
        l = jnp.sum(p, axis=-1, keepdims=True)
        o = jnp.dot(p.astype(BF16), vh, preferred_element_type=F32)
        o_ref[0, :, h * CROSS_HD:(h + 1) * CROSS_HD] = o / l


def cross_attn(q, kv, tq):
    b, t, _ = q.shape
    return pl.pallas_call(
        _cross_kernel,
        out_shape=jax.ShapeDtypeStruct((b, t, CROSS_Q), F32),
        grid=(b, t // tq),
        in_specs=[pl.BlockSpec((1, tq, CROSS_Q), lambda bi, i: (bi, i, 0)),
                  pl.BlockSpec((1, N_MEM, 2 * CROSS_Q), lambda bi, i: (bi, 0, 0))],
        out_specs=pl.BlockSpec((1, tq, CROSS_Q), lambda bi, i: (bi, i, 0)),
        compiler_params=_cparams(("parallel", "parallel")),
        name="cross_attn",
    )(q, kv)


def _silu(z):
    return z / (1.0 + jnp.exp(-z))


def _sigmoid(z):
    return 1.0 / (1.0 + jnp.exp(-z))


def _merge_kernel(x_ref, of_ref, ob_ref, od_ref, oc_ref, ng_ref, wg_ref, gn_ref, wb_ref, wo_ref, fg_ref, o_ref,
                  *, final):
    x = x_ref[...]
    h = ((x * lax.rsqrt(jnp.mean(x * x, axis=-1, keepdims=True) + EPS)) * ng_ref[...]).astype(BF16)
    og = of_ref[...] + ob_ref[...]
    parts = []
    for hd in range(GLA_HEADS):
        oh = og[:, hd * GLA_DV:(hd + 1) * GLA_DV]
        parts.append(oh * lax.rsqrt(jnp.mean(oh * oh, axis=-1, keepdims=True) + EPS) * gn_ref[...])
    branches = (jnp.concatenate(parts, axis=1), od_ref[...], oc_ref[...])
    merged = None
    for i in range(N_BRANCH):
        z = jnp.dot(h, wg_ref[:, i * BRANCH_W:(i + 1) * BRANCH_W], preferred_element_type=F32)
        a = (branches[i] * _silu(z)).astype(BF16)
        y = jnp.dot(a, wb_ref[i], preferred_element_type=F32)
        c0 = N_BRANCH * BRANCH_W + i * D_MODEL
        gate = _sigmoid(jnp.dot(h, wg_ref[:, c0:c0 + D_MODEL], preferred_element_type=F32))
        merged = gate * y if merged is None else merged + gate * y
    x = x + jnp.dot(merged.astype(BF16), wo_ref[...], preferred_element_type=F32)
    if final:
        x = (x * lax.rsqrt(jnp.mean(x * x, axis=-1, keepdims=True) + EPS)) * fg_ref[...]
    o_ref[...] = x


def merge(x2d, of, ob, od, oc, norm_g, w_gates, gla_norm_g, wb, wo, final_g, final, tm):
    m, d = x2d.shape
    row = lambda w: pl.BlockSpec((tm, w), lambda i: (i, 0))
    return pl.pallas_call(
        functools.partial(_merge_kernel, final=final),
        out_shape=jax.ShapeDtypeStruct((m, d), F32),
        grid=(m // tm,),
        in_specs=[row(d), row(BRANCH_W), row(BRANCH_W), row(BRANCH_W), row(BRANCH_W),
                  _resident((1, d)), _resident(w_gates.shape), _resident((1, GLA_DV)),
                  _resident(wb.shape), _resident(wo.shape), _resident((1, d))],
        out_specs=row(d),
        compiler_params=_cparams(("parallel",)),
        name="merge_final" if final else "merge",
    )(x2d, of, ob, od, oc, norm_g.reshape(1, d).astype(F32), w_gates,
      gla_norm_g.reshape(1, GLA_DV).astype(F32), wb, wo, final_g.reshape(1, d).astype(F32))


def _split_w_in(w_in_l):
    offs = np.concatenate([[0], np.cumsum(np.array(SPLITS))])
    col = lambda i: w_in_l[:, int(offs[i]):int(offs[i + 1])]
    g_q, g_k, g_v, g_lr, z_a, d_q, d_k, d_v, z_b, c_q, z_c, gate_in = (col(i) for i in range(len(SPLITS)))
    lr_pad = jnp.pad(g_lr, ((0, 0), (0, GLR_PAD - 2 * GLA_RANK)))
    w_branch_in = jnp.concatenate([g_q, g_k, g_v, lr_pad, d_q, d_k, d_v, c_q], axis=1).astype(BF16)
    w_gates = jnp.concatenate([z_a, z_b, z_c, gate_in], axis=1).astype(BF16)
    return w_branch_in, w_gates


def _gate_weights(w2_l, b2_l, direction):
    w2pad = jnp.zeros((GLR_PAD, GLA_Q), F32)
    w2pad = w2pad.at[direction * GLA_RANK:(direction + 1) * GLA_RANK].set(w2_l[direction].astype(F32))
    return w2pad.astype(BF16), b2_l[direction].reshape(1, GLA_Q).astype(F32)


def _tiles(t):
    return dict(tm=512, tb=min(t, 512), tq=min(t // 2, 512), tc=min(t, 512))


def _trunk(x, mem, norm_g, w_in, gla_gate_w2, gla_gate_b, gla_norm_g, diff_lambda, diff_norm_g,
           mem_norm_g, w_mem_kv, w_branch, w_out, final_norm_g):
    b, t, d = x.shape
    tl = _tiles(t)
    x2d = x.reshape(b * t, d)
    mem2d = mem.reshape(b * N_MEM, d)
    for l in range(DEPTH):
        lambda_init = 0.8 - 0.6 * math.exp(-0.3 * l)
        w_branch_in, w_gates = _split_w_in(w_in[l])
        g_in, qkv, cq = branch_inputs(x2d, norm_g[l], w_branch_in, tl["tm"])
        g_in, qkv, cq = g_in.reshape(b, t, GLA_IN), qkv.reshape(b, t, 3 * DIFF_QK), cq.reshape(b, t, CROSS_Q)
        kv = norm_matmul(mem2d, mem_norm_g[l], w_mem_kv[l].astype(BF16), BF16, N_MEM, 2 * CROSS_Q)
        kv = kv.reshape(b, N_MEM, 2 * CROSS_Q)
        o_f = gla(g_in, *_gate_weights(gla_gate_w2[l], gla_gate_b[l], 0), False, tl["tb"])
        o_b = gla(g_in, *_gate_weights(gla_gate_w2[l], gla_gate_b[l], 1), True, tl["tb"])
        o_d = diff_attn(qkv, diff_lambda[l], diff_norm_g[l], lambda_init, tl["tq"])
        o_c = cross_attn(cq, kv, tl["tc"])
        x2d = merge(x2d, o_f.reshape(b * t, GLA_V), o_b.reshape(b * t, GLA_V), o_d.reshape(b * t, DIFF_V),
                    o_c.reshape(b * t, CROSS_Q), norm_g[l], w_gates, gla_norm_g[l], w_branch[l].astype(BF16),
                    w_out[l].astype(BF16), final_norm_g, l == DEPTH - 1, tl["tm"])
    return x2d.reshape(b, t, d)


def kernel(x_prompt, x_sample, mem_prompt, mem_sample, norm_g, w_in, gla_gate_w2, gla_gate_b, gla_norm_g,
           diff_lambda, diff_norm_g, mem_norm_g, w_mem_kv, w_branch, w_out, final_norm_g):
    params = (norm_g, w_in, gla_gate_w2, gla_gate_b, gla_norm_g, diff_lambda, diff_norm_g, mem_norm_g,
              w_mem_kv, w_branch, w_out, final_norm_g)
    return (_trunk(x_prompt, mem_prompt, *params), _trunk(x_sample, mem_sample, *params))
```

```python
import functools
import math

import numpy as np
import jax
import jax.numpy as jnp
from jax import lax
from jax.experimental import pallas as pl
from jax.experimental.pallas import tpu as pltpu

F32 = jnp.float32
BF16 = jnp.bfloat16

D_MODEL = 1024
DEPTH = 4
N_MEM = 256
BRANCH_W = 512
N_BRANCH = 3
EPS = 1e-6
GLA_HEADS = 4
GLA_DK = 64
GLA_DV = 128
GLA_RANK = 16
GLA_TAU = 16.0
GLA_Q = GLA_HEADS * GLA_DK
GLA_V = GLA_HEADS * GLA_DV
DIFF_HEADS = 4
DIFF_HD = 64
DIFF_DV = 2 * DIFF_HD
DIFF_QK = DIFF_HEADS * 2 * DIFF_HD
DIFF_V = DIFF_HEADS * DIFF_DV
CROSS_HEADS = 4
CROSS_HD = 128
CROSS_Q = CROSS_HEADS * CROSS_HD
SPLITS = (GLA_Q, GLA_Q, GLA_V, 2 * GLA_RANK, BRANCH_W,
          DIFF_QK, DIFF_QK, DIFF_V, BRANCH_W,
          CROSS_Q, BRANCH_W, N_BRANCH * D_MODEL)

LANES = 128
VMEM_LIMIT = 56 * 1024 * 1024
GLA_CHUNK = 128
GLA_LEVELS = 7
GLR_PAD = LANES
GLA_IN = 2 * GLA_Q + GLA_V + GLR_PAD

_NT = (((1,), (1,)), ((), ()))
_TN = (((0,), (0,)), ((), ()))


def _cparams(sem):
    return pltpu.CompilerParams(dimension_semantics=sem, vmem_limit_bytes=VMEM_LIMIT)


def _resident(shape):
    return pl.BlockSpec(shape, lambda *_: (0,) * len(shape), pipeline_mode=pl.Buffered(1))


def _norm_matmul_kernel(x_ref, g_ref, w_ref, o_ref, h_ref):
    @pl.when(pl.program_id(1) == 0)
    def _():
        x = x_ref[...]
        y = x * lax.rsqrt(jnp.mean(x * x, axis=-1, keepdims=True) + EPS)
        h_ref[...] = (y * g_ref[...]).astype(BF16)

    o_ref[...] = jnp.dot(h_ref[...], w_ref[...], preferred_element_type=F32).astype(o_ref.dtype)


def norm_matmul(x2d, g, w_bf16, out_dtype, tm, tn):
    m, d = x2d.shape
    n = w_bf16.shape[1]
    return pl.pallas_call(
        _norm_matmul_kernel,
        out_shape=jax.ShapeDtypeStruct((m, n), out_dtype),
        grid=(m // tm, n // tn),
        in_specs=[pl.BlockSpec((tm, d), lambda i, j: (i, 0)),
                  pl.BlockSpec((1, d), lambda i, j: (0, 0)),
                  pl.BlockSpec((d, tn), lambda i, j: (0, j))],
        out_specs=pl.BlockSpec((tm, tn), lambda i, j: (i, j)),
        scratch_shapes=[pltpu.VMEM((tm, d), BF16)],
        compiler_params=_cparams(("parallel", "arbitrary")),
        name="norm_matmul",
    )(x2d, g.reshape(1, d).astype(F32), w_bf16)


def _proj_kernel(x_ref, g_ref, w_ref, gla_ref, qkv_ref, cq_ref):
    x = x_ref[...]
    h = ((x * lax.rsqrt(jnp.mean(x * x, axis=-1, keepdims=True) + EPS)) * g_ref[...]).astype(BF16)
    c1, c2 = GLA_IN, GLA_IN + 3 * DIFF_QK
    gla_ref[...] = jnp.dot(h, w_ref[:, 0:c1], preferred_element_type=F32)
    qkv_ref[...] = jnp.dot(h, w_ref[:, c1:c2], preferred_element_type=F32).astype(BF16)
    cq_ref[...] = jnp.dot(h, w_ref[:, c2:c2 + CROSS_Q], preferred_element_type=F32).astype(BF16)


def branch_inputs(x2d, g, w_bf16, tm):
    m, d = x2d.shape
    row = lambda w: pl.BlockSpec((tm, w), lambda i: (i, 0))
    return pl.pallas_call(
        _proj_kernel,
        out_shape=(jax.ShapeDtypeStruct((m, GLA_IN), F32),
                   jax.ShapeDtypeStruct((m, 3 * DIFF_QK), BF16),
                   jax.ShapeDtypeStruct((m, CROSS_Q), BF16)),
        grid=(m // tm,),
        in_specs=[row(d), _resident((1, d)), _resident(w_bf16.shape)],
        out_specs=(row(GLA_IN), row(3 * DIFF_QK), row(CROSS_Q)),
        compiler_params=_cparams(("parallel",)),
        name="branch_inputs",
    )(x2d, g.reshape(1, d).astype(F32), w_bf16)


def _gla_constants(reverse):
    c = GLA_CHUNK
    t = np.arange(c)[:, None]
    u = np.arange(c)[None, :]
    blocks = [(u <= t).astype(np.float32), (u > t).astype(np.float32)]
    masks = [(t == u).astype(np.float32)]
    for lvl in range(1, GLA_LEVELS + 1):
        bs, half = 1 << lvl, 1 << (lvl - 1)
        start = (t // bs) * bs
        boundary = start + half - 1
        right = (t % bs) >= half
        m = np.where(right, (u > boundary) & (u <= t), (u > t) & (u <= boundary))
        blocks.append(m.astype(np.float32))
        s = np.arange(c)[None, :]
        same = (s // bs) == (t // bs)
        masks.append((same & right & ((s % bs) < half)).astype(np.float32))
    mall = np.concatenate(blocks, axis=0)
    mask = np.stack(masks, axis=0)
    if reverse:
        mall = mall.reshape(-1, c, c)[:, ::-1, ::-1].reshape(-1, c)
        mask = mask[:, ::-1, ::-1]
    mask = np.concatenate([mask, mask], axis=-1)
    return jnp.asarray(mall, BF16), jnp.asarray(np.ascontiguousarray(mask), F32)


def _gla_kernel(g_ref, mall_ref, mask_ref, w2_ref, b2_ref, o_ref, st_ref, *, reverse, n_chunks):
    c = GLA_CHUNK

    @pl.when(pl.program_id(1) == 0)
    def _():
        st_ref[...] = jnp.zeros_like(st_ref)

    lane = lax.broadcasted_iota(jnp.int32, (c, LANES), 1)
    lane_v = lax.broadcasted_iota(jnp.int32, (c, 2 * GLA_DV), 1)
    lo, hi = lane < GLA_DK, lane >= GLA_DK
    vlo, vhi = lane_v < GLA_DV, lane_v >= GLA_DV
    srow = lax.broadcasted_iota(jnp.int32, (2 * GLA_DV, LANES), 0)
    slane = lax.broadcasted_iota(jnp.int32, (2 * GLA_DV, LANES), 1)
    state_mask = (srow < GLA_DV) == (slane < GLA_DK)
    end_row = 0 if reverse else c - 1

    def chunk(ci, carry):
        idx = (n_chunks - 1 - ci) if reverse else ci
        rows = pl.ds(pl.multiple_of(idx * c, c), c)
        glr = g_ref[0, rows, 2 * GLA_Q + GLA_V:GLA_IN]
        logits = jnp.dot(glr.astype(BF16), w2_ref[...], preferred_element_type=F32) + b2_ref[...]
        log_a = (jnp.minimum(logits, 0.0) - jnp.log1p(jnp.exp(-jnp.abs(logits)))) * (1.0 / GLA_TAU)
        e_all = jnp.exp(jnp.dot(mall_ref[...], log_a.astype(BF16), preferred_element_type=F32))
        for p in range(GLA_HEADS // 2):
            lanes = slice(p * LANES, (p + 1) * LANES)
            q = g_ref[0, rows, p * LANES:(p + 1) * LANES] * (GLA_DK ** -0.5)
            k = g_ref[0, rows, GLA_Q + p * LANES:GLA_Q + (p + 1) * LANES]
            v = g_ref[0, rows, 2 * GLA_Q + p * 2 * GLA_DV:2 * GLA_Q + (p + 1) * 2 * GLA_DV]
            v2 = jnp.concatenate([jnp.where(vlo, v, 0.0), jnp.where(vhi, v, 0.0)], axis=0).astype(BF16)
            e_start = e_all[0:c, lanes]
            e_end = e_all[c:2 * c, lanes]
            st = st_ref[p]
            o = lax.dot_general((q * e_start).astype(BF16), st.astype(BF16), _NT,
                                preferred_element_type=F32)
            att = jnp.zeros((c, 2 * c), F32)
            for lvl in range(GLA_LEVELS + 1):
                if lvl == 0:
                    qt, kt = q, k
                else:
                    e = e_all[(lvl + 1) * c:(lvl + 2) * c, lanes]
                    qt, kt = q * e, k * e
                k2 = jnp.concatenate([jnp.where(lo, kt, 0.0), jnp.where(hi, kt, 0.0)], axis=0)
                a = lax.dot_general(qt.astype(BF16), k2.astype(BF16), _NT, preferred_element_type=F32)
                att = att + a * mask_ref[lvl]
            o = o + jnp.dot(att.astype(BF16), v2, preferred_element_type=F32)
            o_ref[0, rows, p * 2 * GLA_DV:(p + 1) * 2 * GLA_DV] = o
            upd = lax.dot_general(v.astype(BF16), (k * e_end).astype(BF16), _TN,
                                  preferred_element_type=F32)
            decay = e_start[end_row:end_row + 1, :]
            st_ref[p] = st * decay + jnp.where(state_mask, upd, 0.0)
        return carry

    lax.fori_loop(0, n_chunks, chunk, 0, unroll=4)


def gla(g_in, w2pad, b2, reverse, tb):
    b, t, _ = g_in.shape
    nblk = t // tb
    mall, mask = _gla_constants(reverse)
    tmap = (lambda bi, i: (bi, nblk - 1 - i, 0)) if reverse else (lambda bi, i: (bi, i, 0))
    return pl.pallas_call(
        functools.partial(_gla_kernel, reverse=reverse, n_chunks=tb // GLA_CHUNK),
        out_shape=jax.ShapeDtypeStruct((b, t, GLA_V), F32),
        grid=(b, nblk),
        in_specs=[pl.BlockSpec((1, tb, GLA_IN), tmap),
                  _resident(mall.shape), _resident(mask.shape), _resident(w2pad.shape), _resident(b2.shape)],
        out_specs=pl.BlockSpec((1, tb, GLA_V), tmap),
        scratch_shapes=[pltpu.VMEM((GLA_HEADS // 2, 2 * GLA_DV, LANES), F32)],
        compiler_params=_cparams(("parallel", "arbitrary")),
        name="gla_bwd" if reverse else "gla_fwd",
    )(g_in, mall, mask, w2pad, b2)


def _alibi_constants(t, tq):
    slopes = 2.0 ** (-8.0 * (np.arange(DIFF_HEADS, dtype=np.float64) + 1.0) / DIFF_HEADS)
    pos = np.arange(t)
    hi_part = (pos // LANES) * LANES
    lo_part = pos % LANES
    kaug = np.zeros((DIFF_HEADS, t, LANES), np.float32)
    qaug = np.zeros((DIFF_HEADS, t, LANES), np.float32)
    for h in range(DIFF_HEADS):
        kaug[h, :, 0] = -slopes[h] * hi_part
        kaug[h, :, 1] = -slopes[h] * lo_part
        kaug[h, :, 2] = 1.0
        kaug[h, :, 3] = 1.0
        qaug[h, :, 0] = 1.0
        qaug[h, :, 1] = 1.0
        qaug[h, :, 2] = slopes[h] * hi_part
        qaug[h, :, 3] = slopes[h] * lo_part
    i = np.arange(tq)
    dtab = -slopes[:, None, None] * np.abs(i[:, None] - i[None, :])[None]
    dtab = np.concatenate([dtab, dtab], axis=1)
    return jnp.asarray(kaug, BF16), jnp.asarray(qaug, BF16), jnp.asarray(dtab, F32)


def _diff_kernel(q_ref, qn_ref, k_ref, v_ref, kaug_ref, qaug_ref, dtab_ref, lp_ref, gn_ref, o_ref,
                 kp_ref, vp_ref, lhs_ref, acc_ref, s_ref, *, tq, nq, lambda_init):
    qi = pl.program_id(2)
    nk = nq
    lane = lax.broadcasted_iota(jnp.int32, (tq, LANES), 1)

    def stacked_q(ref):
        qs = ref[0] * (DIFF_HD ** -0.5)
        return jnp.concatenate([jnp.where(lane < DIFF_HD, qs, 0.0), jnp.where(lane >= DIFF_HD, qs, 0.0)],
                               axis=0).astype(BF16)

    def key_rows(kj):
        return pl.ds(pl.multiple_of(kj * tq, tq), tq)

    def diagonal_scores(ref, kj):
        lhs = jnp.concatenate([stacked_q(ref), jnp.zeros((2 * tq, LANES), BF16)], axis=1)
        return lax.dot_general(lhs, kp_ref[key_rows(kj), :], _NT, preferred_element_type=F32) + dtab_ref[0]

    @pl.when(qi == 0)
    def _():
        kp_ref[:, 0:LANES] = k_ref[0]
        kp_ref[:, LANES:2 * LANES] = kaug_ref[0]
        vp_ref[:, 0:LANES] = v_ref[0]
        vp_ref[:, LANES:2 * LANES] = jnp.ones((vp_ref.shape[0], LANES), BF16)
        s_ref[0] = diagonal_scores(q_ref, 0)

    qa = jnp.concatenate([qaug_ref[0], qaug_ref[0]], axis=0)
    lhs_ref[:, :, 0:LANES] = jnp.broadcast_to(stacked_q(q_ref)[None], (2, 2 * tq, LANES))
    lhs_ref[0, :, LANES:2 * LANES] = -qa
    lhs_ref[1, :, LANES:2 * LANES] = qa
    acc_ref[...] = jnp.zeros_like(acc_ref)

    def accumulate(s, kj, m):
        m_new = jnp.maximum(m, jnp.max(s, axis=-1, keepdims=True))
        p = jnp.exp(s - m_new)
        acc_ref[...] = acc_ref[...] * jnp.exp(m - m_new) + jnp.dot(p.astype(BF16), vp_ref[key_rows(kj), :],
                                                                  preferred_element_type=F32)
        return m_new

    def block(n):
        return jnp.where(n == 0, qi, jnp.where(n - 1 < qi, n - 1, n))

    def other_scores(n):
        kj = block(n)
        return lax.dot_general(lhs_ref[jnp.where(kj < qi, 0, 1)], kp_ref[key_rows(kj), :], _NT,
                               preferred_element_type=F32)

    assert nk % 2 == 0

    def pair(t, m):
        s_ref[1] = other_scores(2 * t + 1)
        m = accumulate(s_ref[0], block(2 * t), m)
        s_ref[0] = other_scores(2 * t + 2)
        return accumulate(s_ref[1], block(2 * t + 1), m)

    m = lax.fori_loop(0, nk // 2 - 1, pair, jnp.full((2 * tq, 1), -jnp.inf, F32))
    s_ref[1] = other_scores(nk - 1)
    m = accumulate(s_ref[0], block(nk - 2), m)
    s_ref[0] = diagonal_scores(qn_ref, jnp.minimum(qi + 1, nq - 1))
    accumulate(s_ref[1], block(nk - 1), m)

    lp = lp_ref[...]
    lam = (jnp.exp(jnp.sum(lp[0:1] * lp[1:2], axis=-1, keepdims=True))
           - jnp.exp(jnp.sum(lp[2:3] * lp[3:4], axis=-1, keepdims=True)) + lambda_init)
    a1, a2 = acc_ref[0:tq], acc_ref[tq:2 * tq]
    o = a1[:, 0:LANES] / a1[:, LANES:2 * LANES] - lam * (a2[:, 0:LANES] / a2[:, LANES:2 * LANES])
    y = o * lax.rsqrt(jnp.mean(o * o, axis=-1, keepdims=True) + EPS)
    o_ref[0] = (y * gn_ref[...]) * (1.0 - lambda_init)


def diff_attn(qkv, lam_params, g_norm, lambda_init, tq):
    b, t, _ = qkv.shape
    nq = t // tq
    kaug, qaug, dtab = _alibi_constants(t, tq)
    h_ = DIFF_HEADS
    return pl.pallas_call(
        functools.partial(_diff_kernel, tq=tq, nq=nq, lambda_init=lambda_init),
        out_shape=jax.ShapeDtypeStruct((b, t, DIFF_V), F32),
        grid=(b, h_, nq),
        in_specs=[pl.BlockSpec((1, tq, LANES), lambda bi, h, qi: (bi, qi, h)),
                  pl.BlockSpec((1, tq, LANES), lambda bi, h, qi: (bi, jnp.minimum(qi + 1, nq - 1), h)),
                  pl.BlockSpec((1, t, LANES), lambda bi, h, qi: (bi, 0, h_ + h)),
                  pl.BlockSpec((1, t, LANES), lambda bi, h, qi: (bi, 0, 2 * h_ + h)),
                  pl.BlockSpec((1, t, LANES), lambda bi, h, qi: (h, 0, 0)),
                  pl.BlockSpec((1, tq, LANES), lambda bi, h, qi: (h, qi, 0)),
                  pl.BlockSpec((1, 2 * tq, tq), lambda bi, h, qi: (h, 0, 0)),
                  pl.BlockSpec((4, DIFF_HD), lambda bi, h, qi: (0, 0)),
                  pl.BlockSpec((1, DIFF_DV), lambda bi, h, qi: (0, 0))],
        out_specs=pl.BlockSpec((1, tq, LANES), lambda bi, h, qi: (bi, qi, h)),
        scratch_shapes=[pltpu.VMEM((t, 2 * LANES), BF16),
                        pltpu.VMEM((t, 2 * LANES), BF16),
                        pltpu.VMEM((2, 2 * tq, 2 * LANES), BF16),
                        pltpu.VMEM((2 * tq, 2 * LANES), F32),
                        pltpu.VMEM((2, 2 * tq, tq), F32)],
        compiler_params=_cparams(("parallel", "parallel", "arbitrary")),
        name="diff_attn",
    )(qkv, qkv, qkv, qkv, kaug, qaug, dtab, lam_params.astype(F32), g_norm.reshape(1, DIFF_DV).astype(F32))


def _cross_kernel(q_ref, kv_ref, o_ref):
    for h in range(CROSS_HEADS):
        qh = q_ref[0, :, h * CROSS_HD:(h + 1) * CROSS_HD]
        kh = kv_ref[0, :, h * CROSS_HD:(h + 1) * CROSS_HD]
        vh = kv_ref[0, :, CROSS_Q + h * CROSS_HD:CROSS_Q + (h + 1) * CROSS_HD]
        s = lax.dot_general(qh, kh, _NT, preferred_element_type=F32) * (CROSS_HD ** -0.5)
        p = jnp.exp(s - jnp.max(s, axis=-1, keepdims=True))
        l = jnp.sum(p, axis=-1, keepdims=True)
        o = jnp.dot(p.astype(BF16), vh, preferred_element_type=F32)
        o_ref[0, :, h * CROSS_HD:(h + 1) * CROSS_HD] = o / l


def cross_attn(q, kv, tq):
    b, t, _ = q.shape
    return pl.pallas_call(
        _cross_kernel,
        out_shape=jax.ShapeDtypeStruct((b, t, CROSS_Q), F32),
        grid=(b, t // tq),
        in_specs=[pl.BlockSpec((1, tq, CROSS_Q), lambda bi, i: (bi, i, 0)),
                  pl.BlockSpec((1, N_MEM, 2 * CROSS_Q), lambda bi, i: (bi, 0, 0))],
        out_specs=pl.BlockSpec((1, tq, CROSS_Q), lambda bi, i: (bi, i, 0)),
        compiler_params=_cparams(("parallel", "parallel")),
        name="cross_attn",
    )(q, kv)


def _silu(z):
    return z / (1.0 + jnp.exp(-z))


def _sigmoid(z):
    return 1.0 / (1.0 + jnp.exp(-z))


def _merge_kernel(x_ref, of_ref, ob_ref, od_ref, oc_ref, ng_ref, wg_ref, gn_ref, wb_ref, wo_ref, fg_ref, o_ref,
                  *, final):
    x = x_ref[...]
    h = ((x * lax.rsqrt(jnp.mean(x * x, axis=-1, keepdims=True) + EPS)) * ng_ref[...]).astype(BF16)
    og = of_ref[...] + ob_ref[...]
    parts = []
    for hd in range(GLA_HEADS):
        oh = og[:, hd * GLA_DV:(hd + 1) * GLA_DV]
        parts.append(oh * lax.rsqrt(jnp.mean(oh * oh, axis=-1, keepdims=True) + EPS) * gn_ref[...])
    branches = (jnp.concatenate(parts, axis=1), od_ref[...], oc_ref[...])
    merged = None
    for i in range(N_BRANCH):
        z = jnp.dot(h, wg_ref[:, i * BRANCH_W:(i + 1) * BRANCH_W], preferred_element_type=F32)
        a = (branches[i] * _silu(z)).astype(BF16)
        y = jnp.dot(a, wb_ref[i], preferred_element_type=F32)
        c0 = N_BRANCH * BRANCH_W + i * D_MODEL
        gate = _sigmoid(jnp.dot(h, wg_ref[:, c0:c0 + D_MODEL], preferred_element_type=F32))
        merged = gate * y if merged is None else merged + gate * y
    x = x + jnp.dot(merged.astype(BF16), wo_ref[...], preferred_element_type=F32)
    if final:
        x = (x * lax.rsqrt(jnp.mean(x * x, axis=-1, keepdims=True) + EPS)) * fg_ref[...]
    o_ref[...] = x


def merge(x2d, of, ob, od, oc, norm_g, w_gates, gla_norm_g, wb, wo, final_g, final, tm):
    m, d = x2d.shape
    row = lambda w: pl.BlockSpec((tm, w), lambda i: (i, 0))
    return pl.pallas_call(
        functools.partial(_merge_kernel, final=final),
        out_shape=jax.ShapeDtypeStruct((m, d), F32),
        grid=(m // tm,),
        in_specs=[row(d), row(BRANCH_W), row(BRANCH_W), row(BRANCH_W), row(BRANCH_W),
                  _resident((1, d)), _resident(w_gates.shape), _resident((1, GLA_DV)),
                  _resident(wb.shape), _resident(wo.shape), _resident((1, d))],
        out_specs=row(d),
        compiler_params=_cparams(("parallel",)),
        name="merge_final" if final else "merge",
    )(x2d, of, ob, od, oc, norm_g.reshape(1, d).astype(F32), w_gates,
      gla_norm_g.reshape(1, GLA_DV).astype(F32), wb, wo, final_g.reshape(1, d).astype(F32))


def _split_w_in(w_in_l):
    offs = np.concatenate([[0], np.cumsum(np.array(SPLITS))])
    col = lambda i: w_in_l[:, int(offs[i]):int(offs[i + 1])]
    g_q, g_k, g_v, g_lr, z_a, d_q, d_k, d_v, z_b, c_q, z_c, gate_in = (col(i) for i in range(len(SPLITS)))
    lr_pad = jnp.pad(g_lr, ((0, 0), (0, GLR_PAD - 2 * GLA_RANK)))
    w_branch_in = jnp.concatenate([g_q, g_k, g_v, lr_pad, d_q, d_k, d_v, c_q], axis=1).astype(BF16)
    w_gates = jnp.concatenate([z_a, z_b, z_c, gate_in], axis=1).astype(BF16)
    return w_branch_in, w_gates


def _gate_weights(w2_l, b2_l, direction):
    w2pad = jnp.zeros((GLR_PAD, GLA_Q), F32)
    w2pad = w2pad.at[direction * GLA_RANK:(direction + 1) * GLA_RANK].set(w2_l[direction].astype(F32))
    return w2pad.astype(BF16), b2_l[direction].reshape(1, GLA_Q).astype(F32)


def _tiles(t):
    return dict(tm=512, tb=min(t, 512), tq=min(t // 2, 512), tc=min(t, 512))


def _trunk(x, mem, norm_g, w_in, gla_gate_w2, gla_gate_b, gla_norm_g, diff_lambda, diff_norm_g,
           mem_norm_g, w_mem_kv, w_branch, w_out, final_norm_g):
    b, t, d = x.shape
    tl = _tiles(t)
    x2d = x.reshape(b * t, d)
    mem2d = mem.reshape(b * N_MEM, d)
    for l in range(DEPTH):
        lambda_init = 0.8 - 0.6 * math.exp(-0.3 * l)
        w_branch_in, w_gates = _split_w_in(w_in[l])
        g_in, qkv, cq = branch_inputs(x2d, norm_g[l], w_branch_in, tl["tm"])
        g_in, qkv, cq = g_in.reshape(b, t, GLA_IN), qkv.reshape(b, t, 3 * DIFF_QK), cq.reshape(b, t, CROSS_Q)
        kv = norm_matmul(mem2d, mem_norm_g[l], w_mem_kv[l].astype(BF16), BF16, N_MEM, 2 * CROSS_Q)
        kv = kv.reshape(b, N_MEM, 2 * CROSS_Q)
        o_f = gla(g_in, *_gate_weights(gla_gate_w2[l], gla_gate_b[l], 0), False, tl["tb"])
        o_b = gla(g_in, *_gate_weights(gla_gate_w2[l], gla_gate_b[l], 1), True, tl["tb"])
        o_d = diff_attn(qkv, diff_lambda[l], diff_norm_g[l], lambda_init, tl["tq"])
        o_c = cross_attn(cq, kv, tl["tc"])
        x2d = merge(x2d, o_f.reshape(b * t, GLA_V), o_b.reshape(b * t, GLA_V), o_d.reshape(b * t, DIFF_V),
                    o_c.reshape(b * t, CROSS_Q), norm_g[l], w_gates, gla_norm_g[l], w_branch[l].astype(BF16),
                    w_out[l].astype(BF16), final_norm_g, l == DEPTH - 1, tl["tm"])
    return x2d.reshape(b, t, d)


def kernel(x_prompt, x_sample, mem_prompt, mem_sample, norm_g, w_in, gla_gate_w2, gla_gate_b, gla_norm_g,
           diff_lambda, diff_norm_g, mem_norm_g, w_mem_kv, w_branch, w_out, final_norm_g):
    params = (norm_g, w_in, gla_gate_w2, gla_gate_b, gla_norm_g, diff_lambda, diff_norm_g, mem_norm_g,
              w_mem_kv, w_branch, w_out, final_norm_g)
    return (_trunk(x_prompt, mem_prompt, *params), _trunk(x_sample, mem_sample, *params))
```

```python
import functools
import math

import numpy as np
import jax
import jax.numpy as jnp
from jax import lax
from jax.experimental import pallas as pl
from jax.experimental.pallas import tpu as pltpu

F32 = jnp.float32
BF16 = jnp.bfloat16

D_MODEL = 1024
DEPTH = 4
N_MEM = 256
BRANCH_W = 512
N_BRANCH = 3
EPS = 1e-6
GLA_HEADS = 4
GLA_DK = 64
GLA_DV = 128
GLA_RANK = 16
GLA_TAU = 16.0
GLA_Q = GLA_HEADS * GLA_DK
GLA_V = GLA_HEADS * GLA_DV
DIFF_HEADS = 4
DIFF_HD = 64
DIFF_DV = 2 * DIFF_HD
DIFF_QK = DIFF_HEADS * 2 * DIFF_HD
DIFF_V = DIFF_HEADS * DIFF_DV
CROSS_HEADS = 4
CROSS_HD = 128
CROSS_Q = CROSS_HEADS * CROSS_HD
SPLITS = (GLA_Q, GLA_Q, GLA_V, 2 * GLA_RANK, BRANCH_W,
          DIFF_QK, DIFF_QK, DIFF_V, BRANCH_W,
          CROSS_Q, BRANCH_W, N_BRANCH * D_MODEL)

LANES = 128
VMEM_LIMIT = 56 * 1024 * 1024
GLA_CHUNK = 128
GLA_LEVELS = 7
GLR_PAD = LANES
GLA_IN = 2 * GLA_Q + GLA_V + GLR_PAD

_NT = (((1,), (1,)), ((), ()))
_TN = (((0,), (0,)), ((), ()))


def _cparams(sem):
    return pltpu.CompilerParams(dimension_semantics=sem, vmem_limit_bytes=VMEM_LIMIT)


def _resident(shape):
    return pl.BlockSpec(shape, lambda *_: (0,) * len(shape), pipeline_mode=pl.Buffered(1))


def _norm_matmul_kernel(x_ref, g_ref, w_ref, o_ref, h_ref):
    @pl.when(pl.program_id(1) == 0)
    def _():
        x = x_ref[...]
        y = x * lax.rsqrt(jnp.mean(x * x, axis=-1, keepdims=True) + EPS)
        h_ref[...] = (y * g_ref[...]).astype(BF16)

    o_ref[...] = jnp.dot(h_ref[...], w_ref[...], preferred_element_type=F32).astype(o_ref.dtype)


def norm_matmul(x2d, g, w_bf16, out_dtype, tm, tn):
    m, d = x2d.shape
    n = w_bf16.shape[1]
    return pl.pallas_call(
        _norm_matmul_kernel,
        out_shape=jax.ShapeDtypeStruct((m, n), out_dtype),
        grid=(m // tm, n // tn),
        in_specs=[pl.BlockSpec((tm, d), lambda i, j: (i, 0)),
                  pl.BlockSpec((1, d), lambda i, j: (0, 0)),
                  pl.BlockSpec((d, tn), lambda i, j: (0, j))],
        out_specs=pl.BlockSpec((tm, tn), lambda i, j: (i, j)),
        scratch_shapes=[pltpu.VMEM((tm, d), BF16)],
        compiler_params=_cparams(("parallel", "arbitrary")),
        name="norm_matmul",
    )(x2d, g.reshape(1, d).astype(F32), w_bf16)


def _proj_kernel(x_ref, g_ref, w_ref, gla_ref, qkv_ref, cq_ref):
    x = x_ref[...]
    h = ((x * lax.rsqrt(jnp.mean(x * x, axis=-1, keepdims=True) + EPS)) * g_ref[...]).astype(BF16)
    c1, c2 = GLA_IN, GLA_IN + 3 * DIFF_QK
    gla_ref[...] = jnp.dot(h, w_ref[:, 0:c1], preferred_element_type=F32)
    qkv_ref[...] = jnp.dot(h, w_ref[:, c1:c2], preferred_element_type=F32).astype(BF16)
    cq_ref[...] = jnp.dot(h, w_ref[:, c2:c2 + CROSS_Q], preferred_element_type=F32).astype(BF16)


def branch_inputs(x2d, g, w_bf16, tm):
    m, d = x2d.shape
    row = lambda w: pl.BlockSpec((tm, w), lambda i: (i, 0))
    return pl.pallas_call(
        _proj_kernel,
        out_shape=(jax.ShapeDtypeStruct((m, GLA_IN), F32),
                   jax.ShapeDtypeStruct((m, 3 * DIFF_QK), BF16),
                   jax.ShapeDtypeStruct((m, CROSS_Q), BF16)),
        grid=(m // tm,),
        in_specs=[row(d), _resident((1, d)), _resident(w_bf16.shape)],
        out_specs=(row(GLA_IN), row(3 * DIFF_QK), row(CROSS_Q)),
        compiler_params=_cparams(("parallel",)),
        name="branch_inputs",
    )(x2d, g.reshape(1, d).astype(F32), w_bf16)


def _gla_constants(reverse):
    c = GLA_CHUNK
    t = np.arange(c)[:, None]
    u = np.arange(c)[None, :]
    blocks = [(u <= t).astype(np.float32), (u > t).astype(np.float32)]
    masks = [(t == u).astype(np.float32)]
    for lvl in range(1, GLA_LEVELS + 1):
        bs, half = 1 << lvl, 1 << (lvl - 1)
        start = (t // bs) * bs
        boundary = start + half - 1
        right = (t % bs) >= half
        m = np.where(right, (u > boundary) & (u <= t), (u > t) & (u <= boundary))
        blocks.append(m.astype(np.float32))
        s = np.arange(c)[None, :]
        same = (s // bs) == (t // bs)
        masks.append((same & right & ((s % bs) < half)).astype(np.float32))
    mall = np.concatenate(blocks, axis=0)
    mask = np.stack(masks, axis=0)
    if reverse:
        mall = mall.reshape(-1, c, c)[:, ::-1, ::-1].reshape(-1, c)
        mask = mask[:, ::-1, ::-1]
    mask = np.concatenate([mask, mask], axis=-1)
    return jnp.asarray(mall, BF16), jnp.asarray(np.ascontiguousarray(mask), F32)


def _gla_kernel(g_ref, mall_ref, mask_ref, w2_ref, b2_ref, o_ref, st_ref, *, reverse, n_chunks):
    c = GLA_CHUNK

    @pl.when(pl.program_id(1) == 0)
    def _():
        st_ref[...] = jnp.zeros_like(st_ref)

    lane = lax.broadcasted_iota(jnp.int32, (c, LANES), 1)
    lane_v = lax.broadcasted_iota(jnp.int32, (c, 2 * GLA_DV), 1)
    lo, hi = lane < GLA_DK, lane >= GLA_DK
    vlo, vhi = lane_v < GLA_DV, lane_v >= GLA_DV
    srow = lax.broadcasted_iota(jnp.int32, (2 * GLA_DV, LANES), 0)
    slane = lax.broadcasted_iota(jnp.int32, (2 * GLA_DV, LANES), 1)
    state_mask = (srow < GLA_DV) == (slane < GLA_DK)
    end_row = 0 if reverse else c - 1

    def chunk(ci, carry):
        idx = (n_chunks - 1 - ci) if reverse else ci
        rows = pl.ds(pl.multiple_of(idx * c, c), c)
        glr = g_ref[0, rows, 2 * GLA_Q + GLA_V:GLA_IN]
        logits = jnp.dot(glr.astype(BF16), w2_ref[...], preferred_element_type=F32) + b2_ref[...]
        log_a = (jnp.minimum(logits, 0.0) - jnp.log1p(jnp.exp(-jnp.abs(logits)))) * (1.0 / GLA_TAU)
        e_all = jnp.exp(jnp.dot(mall_ref[...], log_a.astype(BF16), preferred_element_type=F32))
        for p in range(GLA_HEADS // 2):
            lanes = slice(p * LANES, (p + 1) * LANES)
            q = g_ref[0, rows, p * LANES:(p + 1) * LANES] * (GLA_DK ** -0.5)
            k = g_ref[0, rows, GLA_Q + p * LANES:GLA_Q + (p + 1) * LANES]
            v = g_ref[0, rows, 2 * GLA_Q + p * 2 * GLA_DV:2 * GLA_Q + (p + 1) * 2 * GLA_DV]
            v2 = jnp.concatenate([jnp.where(vlo, v, 0.0), jnp.where(vhi, v, 0.0)], axis=0).astype(BF16)
            e_start = e_all[0:c, lanes]
            e_end = e_all[c:2 * c, lanes]
            st = st_ref[p]
            o = lax.dot_general((q * e_start).astype(BF16), st.astype(BF16), _NT,
                                preferred_element_type=F32)
            att = jnp.zeros((c, 2 * c), F32)
            for lvl in range(GLA_LEVELS + 1):
                if lvl == 0:
                    qt, kt = q, k
                else:
                    e = e_all[(lvl + 1) * c:(lvl + 2) * c, lanes]
                    qt, kt = q * e, k * e
                k2 = jnp.concatenate([jnp.where(lo, kt, 0.0), jnp.where(hi, kt, 0.0)], axis=0)
                a = lax.dot_general(qt.astype(BF16), k2.astype(BF16), _NT, preferred_element_type=F32)
                att = att + a * mask_ref[lvl]
            o = o + jnp.dot(att.astype(BF16), v2, preferred_element_type=F32)
            o_ref[0, rows, p * 2 * GLA_DV:(p + 1) * 2 * GLA_DV] = o
            upd = lax.dot_general(v.astype(BF16), (k * e_end).astype(BF16), _TN,
                                  preferred_element_type=F32)
            decay = e_start[end_row:end_row + 1, :]
            st_ref[p] = st * decay + jnp.where(state_mask, upd, 0.0)
        return carry

    lax.fori_loop(0, n_chunks, chunk, 0, unroll=4)


def gla(g_in, w2pad, b2, reverse, tb):
    b, t, _ = g_in.shape
    nblk = t // tb
    mall, mask = _gla_constants(reverse)
    tmap = (lambda bi, i: (bi, nblk - 1 - i, 0)) if reverse else (lambda bi, i: (bi, i, 0))
    return pl.pallas_call(
        functools.partial(_gla_kernel, reverse=reverse, n_chunks=tb // GLA_CHUNK),
        out_shape=jax.ShapeDtypeStruct((b, t, GLA_V), F32),
        grid=(b, nblk),
        in_specs=[pl.BlockSpec((1, tb, GLA_IN), tmap),
                  _resident(mall.shape), _resident(mask.shape), _resident(w2pad.shape), _resident(b2.shape)],
        out_specs=pl.BlockSpec((1, tb, GLA_V), tmap),
        scratch_shapes=[pltpu.VMEM((GLA_HEADS // 2, 2 * GLA_DV, LANES), F32)],
        compiler_params=_cparams(("parallel", "arbitrary")),
        name="gla_bwd" if reverse else "gla_fwd",
    )(g_in, mall, mask, w2pad, b2)


def _alibi_constants(t, tq):
    slopes = 2.0 ** (-8.0 * (np.arange(DIFF_HEADS, dtype=np.float64) + 1.0) / DIFF_HEADS)
    pos = np.arange(t)
    hi_part = (pos // LANES) * LANES
    lo_part = pos % LANES
    kaug = np.zeros((DIFF_HEADS, t, LANES), np.float32)
    qaug = np.zeros((DIFF_HEADS, t, LANES), np.float32)
    for h in range(DIFF_HEADS):
        kaug[h, :, 0] = -slopes[h] * hi_part
        kaug[h, :, 1] = -slopes[h] * lo_part
        kaug[h, :, 2] = 1.0
        kaug[h, :, 3] = 1.0
        qaug[h, :, 0] = 1.0
        qaug[h, :, 1] = 1.0
        qaug[h, :, 2] = slopes[h] * hi_part
        qaug[h, :, 3] = slopes[h] * lo_part
    i = np.arange(tq)
    dtab = -slopes[:, None, None] * np.abs(i[:, None] - i[None, :])[None]
    dtab = np.concatenate([dtab, dtab], axis=1)
    kaug_t = kaug.reshape(DIFF_HEADS, t // tq, tq, LANES).transpose(0, 1, 3, 2)
    return jnp.asarray(kaug_t, BF16), jnp.asarray(qaug, BF16), jnp.asarray(dtab, F32)


def _diff_kernel(q_ref, qn_ref, k_ref, v_ref, kaug_ref, qaug_ref, dtab_ref, lp_ref, gn_ref, o_ref,
                 kp_ref, vp_ref, lhs_ref, acc_ref, s_ref, *, tq, nq, qps, lambda_init):
    step = pl.program_id(2)
    nk = nq
    assert nk % 2 == 0
    lane = lax.broadcasted_iota(jnp.int32, (tq, LANES), 1)

    def stacked_q(q):
        qs = q * (DIFF_HD ** -0.5)
        return jnp.concatenate([jnp.where(lane < DIFF_HD, qs, 0.0), jnp.where(lane >= DIFF_HD, qs, 0.0)],
                               axis=0).astype(BF16)

    def key_rows(kj):
        return pl.ds(pl.multiple_of(kj * tq, tq), tq)

    def diagonal_scores(q, kj):
        lhs = jnp.concatenate([stacked_q(q), jnp.zeros((2 * tq, LANES), BF16)], axis=1)
        return jnp.dot(lhs, kp_ref[kj], preferred_element_type=F32) + dtab_ref[0]

    @pl.when(step == 0)
    def _():
        for kb in range(nk):
            kp_ref[kb, 0:LANES, :] = k_ref[0, kb * tq:(kb + 1) * tq, :].T
            kp_ref[kb, LANES:2 * LANES, :] = kaug_ref[0, kb]
        vp_ref[:, 0:LANES] = v_ref[0]
        vp_ref[:, LANES:2 * LANES] = jnp.ones((vp_ref.shape[0], LANES), BF16)
        s_ref[0] = diagonal_scores(q_ref[0, 0:tq, :], 0)

    lp = lp_ref[...]
    lam = (jnp.exp(jnp.sum(lp[0:1] * lp[1:2], axis=-1, keepdims=True))
           - jnp.exp(jnp.sum(lp[2:3] * lp[3:4], axis=-1, keepdims=True)) + lambda_init)

    for j in range(qps):
        qi = step * qps + j
        rows_j = slice(j * tq, (j + 1) * tq)
        lhs, acc = lhs_ref.at[j], acc_ref.at[j]
        qa = jnp.concatenate([qaug_ref[0, rows_j, :], qaug_ref[0, rows_j, :]], axis=0)
        lhs[:, :, 0:LANES] = jnp.broadcast_to(stacked_q(q_ref[0, rows_j, :])[None], (2, 2 * tq, LANES))
        lhs[0, :, LANES:2 * LANES] = -qa
        lhs[1, :, LANES:2 * LANES] = qa
        acc[...] = jnp.zeros((2 * tq, 2 * LANES), F32)

        def accumulate(s, kj, m, acc=acc):
            m_new = jnp.maximum(m, jnp.max(s, axis=-1, keepdims=True))
            p = jnp.exp(s - m_new)
            acc[...] = acc[...] * jnp.exp(m - m_new) + jnp.dot(p.astype(BF16), vp_ref[key_rows(kj), :],
                                                                preferred_element_type=F32)
            return m_new

        def block(n, qi=qi):
            return jnp.where(n == 0, qi, jnp.where(n - 1 < qi, n - 1, n))

        def other_scores(n, qi=qi, lhs=lhs, block=block):
            kj = block(n)
            return jnp.dot(lhs[jnp.where(kj < qi, 0, 1)], kp_ref[kj], preferred_element_type=F32)

        def pair(t, m, accumulate=accumulate, block=block, other_scores=other_scores):
            s_ref[1] = other_scores(2 * t + 1)
            m = accumulate(s_ref[0], block(2 * t), m)
            s_ref[0] = other_scores(2 * t + 2)
            return accumulate(s_ref[1], block(2 * t + 1), m)

        m = lax.fori_loop(0, nk // 2 - 1, pair, jnp.full((2 * tq, 1), -jnp.inf, F32), unroll=True)
        s_ref[1] = other_scores(nk - 1)
        m = accumulate(s_ref[0], block(nk - 2), m)
        if j + 1 < qps:
            s_ref[0] = diagonal_scores(q_ref[0, (j + 1) * tq:(j + 2) * tq, :], qi + 1)
        else:
            s_ref[0] = diagonal_scores(qn_ref[0], jnp.minimum(qi + 1, nq - 1))
        accumulate(s_ref[1], block(nk - 1), m)

        a1, a2 = acc[0:tq], acc[tq:2 * tq]
        o = a1[:, 0:LANES] / a1[:, LANES:2 * LANES] - lam * (a2[:, 0:LANES] / a2[:, LANES:2 * LANES])
        y = o * lax.rsqrt(jnp.mean(o * o, axis=-1, keepdims=True) + EPS)
        o_ref[0, rows_j, :] = (y * gn_ref[...]) * (1.0 - lambda_init)


def diff_attn(qkv, lam_params, g_norm, lambda_init, tq, qps):
    b, t, _ = qkv.shape
    nq = t // tq
    ns = nq // qps
    kaug, qaug, dtab = _alibi_constants(t, tq)
    h_ = DIFF_HEADS
    return pl.pallas_call(
        functools.partial(_diff_kernel, tq=tq, nq=nq, qps=qps, lambda_init=lambda_init),
        out_shape=jax.ShapeDtypeStruct((b, t, DIFF_V), F32),
        grid=(b, h_, ns),
        in_specs=[pl.BlockSpec((1, qps * tq, LANES), lambda bi, h, si: (bi, si, h)),
                  pl.BlockSpec((1, tq, LANES), lambda bi, h, si: (bi, jnp.minimum((si + 1) * qps, nq - 1), h)),
                  pl.BlockSpec((1, t, LANES), lambda bi, h, si: (bi, 0, h_ + h)),
                  pl.BlockSpec((1, t, LANES), lambda bi, h, si: (bi, 0, 2 * h_ + h)),
                  pl.BlockSpec((1, nq, LANES, tq), lambda bi, h, si: (h, 0, 0, 0)),
                  pl.BlockSpec((1, qps * tq, LANES), lambda bi, h, si: (h, si, 0)),
                  pl.BlockSpec((1, 2 * tq, tq), lambda bi, h, si: (h, 0, 0)),
                  pl.BlockSpec((4, DIFF_HD), lambda bi, h, si: (0, 0)),
                  pl.BlockSpec((1, DIFF_DV), lambda bi, h, si: (0, 0))],
        out_specs=pl.BlockSpec((1, qps * tq, LANES), lambda bi, h, si: (bi, si, h)),
        scratch_shapes=[pltpu.VMEM((nq, 2 * LANES, tq), BF16),
                        pltpu.VMEM((t, 2 * LANES), BF16),
                        pltpu.VMEM((qps, 2, 2 * tq, 2 * LANES), BF16),
                        pltpu.VMEM((qps, 2 * tq, 2 * LANES), F32),
                        pltpu.VMEM((2, 2 * tq, tq), F32)],
        compiler_params=_cparams(("parallel", "parallel", "arbitrary")),
        name="diff_attn",
    )(qkv, qkv, qkv, qkv, kaug, qaug, dtab, lam_params.astype(F32), g_norm.reshape(1, DIFF_DV).astype(F32))


def _cross_kernel(q_ref, kv_ref, o_ref):
    for h in range(CROSS_HEADS):
        qh = q_ref[0, :, h * CROSS_HD:(h + 1) * CROSS_HD]
        kh = kv_ref[0, :, h * CROSS_HD:(h + 1) * CROSS_HD]
        vh = kv_ref[0, :, CROSS_Q + h * CROSS_HD:CROSS_Q + (h + 1) * CROSS_HD]
        s = lax.dot_general(qh, kh, _NT, preferred_element_type=F32) * (CROSS_HD ** -0.5)
        p = jnp.exp(s - jnp.max(s, axis=-1, keepdims=True))
        l = jnp.sum(p, axis=-1, keepdims=True)
        o = jnp.dot(p.astype(BF16), vh, preferred_element_type=F32)
        o_ref[0, :, h * CROSS_HD:(h + 1) * CROSS_HD] = o / l


def cross_attn(q, kv, tq):
    b, t, _ = q.shape
    return pl.pallas_call(
        _cross_kernel,
        out_shape=jax.ShapeDtypeStruct((b, t, CROSS_Q), F32),
        grid=(b, t // tq),
        in_specs=[pl.BlockSpec((1, tq, CROSS_Q), lambda bi, i: (bi, i, 0)),
                  pl.BlockSpec((1, N_MEM, 2 * CROSS_Q), lambda bi, i: (bi, 0, 0))],
        out_specs=pl.BlockSpec((1, tq, CROSS_Q), lambda bi, i: (bi, i, 0)),
        compiler_params=_cparams(("parallel", "parallel")),
        name="cross_attn",
    )(q, kv)


def _silu(z):
    return z / (1.0 + jnp.exp(-z))


def _sigmoid(z):
    return 1.0 / (1.0 + jnp.exp(-z))


def _merge_kernel(x_ref, of_ref, ob_ref, od_ref, oc_ref, ng_ref, wg_ref, gn_ref, wb_ref, wo_ref, fg_ref, o_ref,
                  *, final):
    x = x_ref[...]
    h = ((x * lax.rsqrt(jnp.mean(x * x, axis=-1, keepdims=True) + EPS)) * ng_ref[...]).astype(BF16)
    og = of_ref[...] + ob_ref[...]
    parts = []
    for hd in range(GLA_HEADS):
        oh = og[:, hd * GLA_DV:(hd + 1) * GLA_DV]
        parts.append(oh * lax.rsqrt(jnp.mean(oh * oh, axis=-1, keepdims=True) + EPS) * gn_ref[...])
    branches = (jnp.concatenate(parts, axis=1), od_ref[...], oc_ref[...])
    merged = None
    for i in range(N_BRANCH):
        z = jnp.dot(h, wg_ref[:, i * BRANCH_W:(i + 1) * BRANCH_W], preferred_element_type=F32)
        a = (branches[i] * _silu(z)).astype(BF16)
        y = jnp.dot(a, wb_ref[i], preferred_element_type=F32)
        c0 = N_BRANCH * BRANCH_W + i * D_MODEL
        gate = _sigmoid(jnp.dot(h, wg_ref[:, c0:c0 + D_MODEL], preferred_element_type=F32))
        merged = gate * y if merged is None else merged + gate * y
    x = x + jnp.dot(merged.astype(BF16), wo_ref[...], preferred_element_type=F32)
    if final:
        x = (x * lax.rsqrt(jnp.mean(x * x, axis=-1, keepdims=True) + EPS)) * fg_ref[...]
    o_ref[...] = x


def merge(x2d, of, ob, od, oc, norm_g, w_gates, gla_norm_g, wb, wo, final_g, final, tm):
    m, d = x2d.shape
    row = lambda w: pl.BlockSpec((tm, w), lambda i: (i, 0))
    return pl.pallas_call(
        functools.partial(_merge_kernel, final=final),
        out_shape=jax.ShapeDtypeStruct((m, d), F32),
        grid=(m // tm,),
        in_specs=[row(d), row(BRANCH_W), row(BRANCH_W), row(BRANCH_W), row(BRANCH_W),
                  _resident((1, d)), _resident(w_gates.shape), _resident((1, GLA_DV)),
                  _resident(wb.shape), _resident(wo.shape), _resident((1, d))],
        out_specs=row(d),
        compiler_params=_cparams(("parallel",)),
        name="merge_final" if final else "merge",
    )(x2d, of, ob, od, oc, norm_g.reshape(1, d).astype(F32), w_gates,
      gla_norm_g.reshape(1, GLA_DV).astype(F32), wb, wo, final_g.reshape(1, d).astype(F32))


def _split_w_in(w_in_l):
    offs = np.concatenate([[0], np.cumsum(np.array(SPLITS))])
    col = lambda i: w_in_l[:, int(offs[i]):int(offs[i + 1])]
    g_q, g_k, g_v, g_lr, z_a, d_q, d_k, d_v, z_b, c_q, z_c, gate_in = (col(i) for i in range(len(SPLITS)))
    lr_pad = jnp.pad(g_lr, ((0, 0), (0, GLR_PAD - 2 * GLA_RANK)))
    w_branch_in = jnp.concatenate([g_q, g_k, g_v, lr_pad, d_q, d_k, d_v, c_q], axis=1).astype(BF16)
    w_gates = jnp.concatenate([z_a, z_b, z_c, gate_in], axis=1).astype(BF16)
    return w_branch_in, w_gates


def _gate_weights(w2_l, b2_l, direction):
    w2pad = jnp.zeros((GLR_PAD, GLA_Q), F32)
    w2pad = w2pad.at[direction * GLA_RANK:(direction + 1) * GLA_RANK].set(w2_l[direction].astype(F32))
    return w2pad.astype(BF16), b2_l[direction].reshape(1, GLA_Q).astype(F32)


def _tiles(t):
    tq = min(t // 2, 512)
    qps = t // tq if t // tq <= 4 else 1
    return dict(tm=512, tb=min(t, 512), tq=tq, qps=qps, tc=min(t, 512))


def _trunk(x, mem, norm_g, w_in, gla_gate_w2, gla_gate_b, gla_norm_g, diff_lambda, diff_norm_g,
           mem_norm_g, w_mem_kv, w_branch, w_out, final_norm_g):
    b, t, d = x.shape
    tl = _tiles(t)
    x2d = x.reshape(b * t, d)
    mem2d = mem.reshape(b * N_MEM, d)
    for l in range(DEPTH):
        lambda_init = 0.8 - 0.6 * math.exp(-0.3 * l)
        w_branch_in, w_gates = _split_w_in(w_in[l])
        g_in, qkv, cq = branch_inputs(x2d, norm_g[l], w_branch_in, tl["tm"])
        g_in, qkv, cq = g_in.reshape(b, t, GLA_IN), qkv.reshape(b, t, 3 * DIFF_QK), cq.reshape(b, t, CROSS_Q)
        kv = norm_matmul(mem2d, mem_norm_g[l], w_mem_kv[l].astype(BF16), BF16, N_MEM, 2 * CROSS_Q)
        kv = kv.reshape(b, N_MEM, 2 * CROSS_Q)
        o_f = gla(g_in, *_gate_weights(gla_gate_w2[l], gla_gate_b[l], 0), False, tl["tb"])
        o_b = gla(g_in, *_gate_weights(gla_gate_w2[l], gla_gate_b[l], 1), True, tl["tb"])
        o_d = diff_attn(qkv, diff_lambda[l], diff_norm_g[l], lambda_init, tl["tq"], tl["qps"])
        o_c = cross_attn(cq, kv, tl["tc"])
        x2d = merge(x2d, o_f.reshape(b * t, GLA_V), o_b.reshape(b * t, GLA_V), o_d.reshape(b * t, DIFF_V),
                    o_c.reshape(b * t, CROSS_Q), norm_g[l], w_gates, gla_norm_g[l], w_branch[l].astype(BF16),
                    w_out[l].astype(BF16), final_norm_g, l == DEPTH - 1, tl["tm"])
    return x2d.reshape(b, t, d)


def kernel(x_prompt, x_sample, mem_prompt, mem_sample, norm_g, w_in, gla_gate_w2, gla_gate_b, gla_norm_g,
           diff_lambda, diff_norm_g, mem_norm_g, w_mem_kv, w_branch, w_out, final_norm_g):
    params = (norm_g, w_in, gla_gate_w2, gla_gate_b, gla_norm_g, diff_lambda, diff_norm_g, mem_norm_g,
              w_mem_kv, w_branch, w_out, final_norm_g)
    return (_trunk(x_prompt, mem_prompt, *params), _trunk(x_sample, mem_sample, *params))
```

```python
import functools
import math

import numpy as np
import jax
import jax.numpy as jnp
from jax import lax
from jax.experimental import pallas as pl
from jax.experimental.pallas import tpu as pltpu

F32 = jnp.float32
BF16 = jnp.bfloat16

D_MODEL = 1024
DEPTH = 4
N_MEM = 256
BRANCH_W = 512
N_BRANCH = 3
EPS = 1e-6
GLA_HEADS = 4
GLA_DK = 64
GLA_DV = 128
GLA_RANK = 16
GLA_TAU = 16.0
GLA_Q = GLA_HEADS * GLA_DK
GLA_V = GLA_HEADS * GLA_DV
DIFF_HEADS = 4
DIFF_HD = 64
DIFF_DV = 2 * DIFF_HD
DIFF_QK = DIFF_HEADS * 2 * DIFF_HD
DIFF_V = DIFF_HEADS * DIFF_DV
CROSS_HEADS = 4
CROSS_HD = 128
CROSS_Q = CROSS_HEADS * CROSS_HD
SPLITS = (GLA_Q, GLA_Q, GLA_V, 2 * GLA_RANK, BRANCH_W,
          DIFF_QK, DIFF_QK, DIFF_V, BRANCH_W,
          CROSS_Q, BRANCH_W, N_BRANCH * D_MODEL)

LANES = 128
VMEM_LIMIT = 56 * 1024 * 1024
GLA_CHUNK = 128
GLA_LEVELS = 7
GLR_PAD = LANES
GLA_IN = 2 * GLA_Q + GLA_V + GLR_PAD

_NT = (((1,), (1,)), ((), ()))
_TN = (((0,), (0,)), ((), ()))


def _cparams(sem):
    return pltpu.CompilerParams(dimension_semantics=sem, vmem_limit_bytes=VMEM_LIMIT)


def _resident(shape):
    return pl.BlockSpec(shape, lambda *_: (0,) * len(shape), pipeline_mode=pl.Buffered(1))


def _norm_matmul_kernel(x_ref, g_ref, w_ref, o_ref, h_ref):
    @pl.when(pl.program_id(1) == 0)
    def _():
        x = x_ref[...]
        y = x * lax.rsqrt(jnp.mean(x * x, axis=-1, keepdims=True) + EPS)
        h_ref[...] = (y * g_ref[...]).astype(BF16)

    o_ref[...] = jnp.dot(h_ref[...], w_ref[...], preferred_element_type=F32).astype(o_ref.dtype)


def norm_matmul(x2d, g, w_bf16, out_dtype, tm, tn):
    m, d = x2d.shape
    n = w_bf16.shape[1]
    return pl.pallas_call(
        _norm_matmul_kernel,
        out_shape=jax.ShapeDtypeStruct((m, n), out_dtype),
        grid=(m // tm, n // tn),
        in_specs=[pl.BlockSpec((tm, d), lambda i, j: (i, 0)),
                  pl.BlockSpec((1, d), lambda i, j: (0, 0)),
                  pl.BlockSpec((d, tn), lambda i, j: (0, j))],
        out_specs=pl.BlockSpec((tm, tn), lambda i, j: (i, j)),
        scratch_shapes=[pltpu.VMEM((tm, d), BF16)],
        compiler_params=_cparams(("parallel", "arbitrary")),
        name="norm_matmul",
    )(x2d, g.reshape(1, d).astype(F32), w_bf16)


def _proj_kernel(x_ref, g_ref, w_ref, gla_ref, qkv_ref, cq_ref):
    x = x_ref[...]
    h = ((x * lax.rsqrt(jnp.mean(x * x, axis=-1, keepdims=True) + EPS)) * g_ref[...]).astype(BF16)
    c1, c2 = GLA_IN, GLA_IN + 3 * DIFF_QK
    gla_ref[...] = jnp.dot(h, w_ref[:, 0:c1], preferred_element_type=F32)
    qkv_ref[...] = jnp.dot(h, w_ref[:, c1:c2], preferred_element_type=F32).astype(BF16)
    cq_ref[...] = jnp.dot(h, w_ref[:, c2:c2 + CROSS_Q], preferred_element_type=F32).astype(BF16)


def branch_inputs(x2d, g, w_bf16, tm):
    m, d = x2d.shape
    row = lambda w: pl.BlockSpec((tm, w), lambda i: (i, 0))
    return pl.pallas_call(
        _proj_kernel,
        out_shape=(jax.ShapeDtypeStruct((m, GLA_IN), F32),
                   jax.ShapeDtypeStruct((m, 3 * DIFF_QK), BF16),
                   jax.ShapeDtypeStruct((m, CROSS_Q), BF16)),
        grid=(m // tm,),
        in_specs=[row(d), _resident((1, d)), _resident(w_bf16.shape)],
        out_specs=(row(GLA_IN), row(3 * DIFF_QK), row(CROSS_Q)),
        compiler_params=_cparams(("parallel",)),
        name="branch_inputs",
    )(x2d, g.reshape(1, d).astype(F32), w_bf16)


def _gla_constants(reverse):
    c = GLA_CHUNK
    t = np.arange(c)[:, None]
    u = np.arange(c)[None, :]
    start, end, total = (u <= t), (u > t), np.ones((c, c), bool)
    levels, masks = [], [(t == u)]
    for lvl in range(1, GLA_LEVELS + 1):
        bs, half = 1 << lvl, 1 << (lvl - 1)
        boundary = (t // bs) * bs + half - 1
        right = (t % bs) >= half
        levels.append(np.where(right, (u > boundary) & (u <= t), (u > t) & (u <= boundary)))
        masks.append(((u // bs) == (t // bs)) & right & ((u % bs) < half))
    flip = (lambda m: m[::-1, ::-1]) if reverse else (lambda m: m)
    mall_t = np.concatenate([flip(m).T for m in [start, end] + levels + [total]], axis=1).astype(np.float32)
    mask = np.stack([flip(m) for m in masks], axis=0).astype(np.float32)
    mask = np.concatenate([mask, mask], axis=-1)
    return jnp.asarray(mall_t, BF16), jnp.asarray(np.ascontiguousarray(mask), F32)


def _log_decay(logits):
    return (jnp.minimum(logits, 0.0) - jnp.log1p(jnp.exp(-jnp.abs(logits)))) * (1.0 / GLA_TAU)


def _gla_kernel(g_ref, mt_ref, mask_ref, w2t_ref, b2t_ref, o_ref, st_ref, *, reverse, n_chunks):
    c = GLA_CHUNK

    @pl.when(pl.program_id(1) == 0)
    def _():
        st_ref[...] = jnp.zeros_like(st_ref)

    lane_v = lax.broadcasted_iota(jnp.int32, (c, 2 * GLA_DV), 1)
    vlo, vhi = lane_v < GLA_DV, lane_v >= GLA_DV
    drow = lax.broadcasted_iota(jnp.int32, (LANES, c), 0)
    dlo, dhi = drow < GLA_DK, drow >= GLA_DK
    state_mask = ((lax.broadcasted_iota(jnp.int32, (LANES, 2 * GLA_DV), 0) < GLA_DK)
                  == (lax.broadcasted_iota(jnp.int32, (LANES, 2 * GLA_DV), 1) < GLA_DV))

    def chunk(ci, carry):
        idx = (n_chunks - 1 - ci) if reverse else ci
        rows = pl.ds(pl.multiple_of(idx * c, c), c)
        glr_t = g_ref[0, rows, 2 * GLA_Q + GLA_V:GLA_IN].T.astype(BF16)
        log_at = _log_decay(jnp.dot(w2t_ref[...], glr_t, preferred_element_type=F32) + b2t_ref[...])
        e_all = jnp.exp(jnp.dot(log_at.astype(BF16), mt_ref[...], preferred_element_type=F32))
        for p in range(GLA_HEADS // 2):
            q_t = g_ref[0, rows, p * LANES:(p + 1) * LANES].T * (GLA_DK ** -0.5)
            k_t = g_ref[0, rows, GLA_Q + p * LANES:GLA_Q + (p + 1) * LANES].T
            v = g_ref[0, rows, 2 * GLA_Q + p * 2 * GLA_DV:2 * GLA_Q + (p + 1) * 2 * GLA_DV]
            v2 = jnp.concatenate([jnp.where(vlo, v, 0.0), jnp.where(vhi, v, 0.0)], axis=0).astype(BF16)
            e = lambda blk: e_all[p * LANES:(p + 1) * LANES, blk * c:(blk + 1) * c]
            st = st_ref[p]
            o = lax.dot_general((q_t * e(0)).astype(BF16), st.astype(BF16), _TN, preferred_element_type=F32)
            att = jnp.zeros((c, 2 * c), F32)
            for lvl in range(GLA_LEVELS + 1):
                qs, ks = (q_t, k_t) if lvl == 0 else (q_t * e(lvl + 1), k_t * e(lvl + 1))
                k2 = jnp.concatenate([jnp.where(dlo, ks, 0.0), jnp.where(dhi, ks, 0.0)], axis=1)
                a = lax.dot_general(qs.astype(BF16), k2.astype(BF16), _TN, preferred_element_type=F32)
                att = att + a * mask_ref[lvl]
            o = o + jnp.dot(att.astype(BF16), v2, preferred_element_type=F32)
            o_ref[0, rows, p * 2 * GLA_DV:(p + 1) * 2 * GLA_DV] = o
            upd = jnp.dot((k_t * e(1)).astype(BF16), v.astype(BF16), preferred_element_type=F32)
            decay = e(GLA_LEVELS + 2)
            st_ref[p] = st * jnp.concatenate([decay, decay], axis=1) + jnp.where(state_mask, upd, 0.0)
        return carry

    lax.fori_loop(0, n_chunks, chunk, 0, unroll=True)


def gla(g_in, w2pad, b2, reverse, tb):
    b, t, _ = g_in.shape
    nblk = t // tb
    mall_t, mask = _gla_constants(reverse)
    b2t = jnp.broadcast_to(b2.reshape(GLA_Q, 1), (GLA_Q, GLA_CHUNK))
    tmap = (lambda bi, i: (bi, nblk - 1 - i, 0)) if reverse else (lambda bi, i: (bi, i, 0))
    consts = (mall_t, mask, w2pad.T, b2t)
    return pl.pallas_call(
        functools.partial(_gla_kernel, reverse=reverse, n_chunks=tb // GLA_CHUNK),
        out_shape=jax.ShapeDtypeStruct((b, t, GLA_V), F32),
        grid=(b, nblk),
        in_specs=[pl.BlockSpec((1, tb, GLA_IN), tmap)] + [_resident(a.shape) for a in consts],
        out_specs=pl.BlockSpec((1, tb, GLA_V), tmap),
        scratch_shapes=[pltpu.VMEM((GLA_HEADS // 2, LANES, 2 * GLA_DV), F32)],
        compiler_params=_cparams(("parallel", "arbitrary")),
        name="gla_bwd" if reverse else "gla_fwd",
    )(g_in, *consts)


def _alibi_constants(t, tq):
    slopes = 2.0 ** (-8.0 * (np.arange(DIFF_HEADS, dtype=np.float64) + 1.0) / DIFF_HEADS)
    pos = np.arange(t)
    hi_part = (pos // LANES) * LANES
    lo_part = pos % LANES
    kaug = np.zeros((DIFF_HEADS, t, LANES), np.float32)
    qaug = np.zeros((DIFF_HEADS, t, LANES), np.float32)
    for h in range(DIFF_HEADS):
        kaug[h, :, 0] = -slopes[h] * hi_part
        kaug[h, :, 1] = -slopes[h] * lo_part
        kaug[h, :, 2] = 1.0
        kaug[h, :, 3] = 1.0
        qaug[h, :, 0] = 1.0
        qaug[h, :, 1] = 1.0
        qaug[h, :, 2] = slopes[h] * hi_part
        qaug[h, :, 3] = slopes[h] * lo_part
    i = np.arange(tq)
    dtab = -slopes[:, None, None] * np.abs(i[:, None] - i[None, :])[None]
    dtab = np.concatenate([dtab, dtab], axis=1)
    kaug_t = kaug.reshape(DIFF_HEADS, t // tq, tq, LANES).transpose(0, 1, 3, 2)
    return jnp.asarray(kaug_t, BF16), jnp.asarray(qaug, BF16), jnp.asarray(dtab, F32)


def _diff_kernel(q_ref, qn_ref, k_ref, v_ref, kaug_ref, qaug_ref, dtab_ref, lp_ref, gn_ref, o_ref,
                 kp_ref, vp_ref, lhs_ref, acc_ref, s_ref, *, tq, nq, qps, lambda_init):
    step = pl.program_id(2)
    nk = nq
    assert nk % 2 == 0
    lane = lax.broadcasted_iota(jnp.int32, (tq, LANES), 1)

    def stacked_q(q):
        qs = q * (DIFF_HD ** -0.5)
        return jnp.concatenate([jnp.where(lane < DIFF_HD, qs, 0.0), jnp.where(lane >= DIFF_HD, qs, 0.0)],
                               axis=0).astype(BF16)

    def key_rows(kj):
        return pl.ds(pl.multiple_of(kj * tq, tq), tq)

    def diagonal_scores(q, kj):
        lhs = jnp.concatenate([stacked_q(q), jnp.zeros((2 * tq, LANES), BF16)], axis=1)
        return jnp.dot(lhs, kp_ref[kj], preferred_element_type=F32) + dtab_ref[0]

    @pl.when(step == 0)
    def _():
        for kb in range(nk):
            kp_ref[kb, 0:LANES, :] = k_ref[0, kb * tq:(kb + 1) * tq, :].T
            kp_ref[kb, LANES:2 * LANES, :] = kaug_ref[0, kb]
        vp_ref[:, 0:LANES] = v_ref[0]
        vp_ref[:, LANES:2 * LANES] = jnp.ones((vp_ref.shape[0], LANES), BF16)
        s_ref[0] = diagonal_scores(q_ref[0, 0:tq, :], 0)

    lp = lp_ref[...]
    lam = (jnp.exp(jnp.sum(lp[0:1] * lp[1:2], axis=-1, keepdims=True))
           - jnp.exp(jnp.sum(lp[2:3] * lp[3:4], axis=-1, keepdims=True)) + lambda_init)

    for j in range(qps):
        qi = step * qps + j
        rows_j = slice(j * tq, (j + 1) * tq)
        lhs, acc = lhs_ref.at[j], acc_ref.at[j]
        qa = jnp.concatenate([qaug_ref[0, rows_j, :], qaug_ref[0, rows_j, :]], axis=0)
        lhs[:, :, 0:LANES] = jnp.broadcast_to(stacked_q(q_ref[0, rows_j, :])[None], (2, 2 * tq, LANES))
        lhs[0, :, LANES:2 * LANES] = -qa
        lhs[1, :, LANES:2 * LANES] = qa
        acc[...] = jnp.zeros((2 * tq, 2 * LANES), F32)

        def accumulate(s, kj, m, acc=acc):
            m_new = jnp.maximum(m, jnp.max(s, axis=-1, keepdims=True))
            p = jnp.exp(s - m_new)
            acc[...] = acc[...] * jnp.exp(m - m_new) + jnp.dot(p.astype(BF16), vp_ref[key_rows(kj), :],
                                                                preferred_element_type=F32)
            return m_new

        def block(n, qi=qi):
            return jnp.where(n == 0, qi, jnp.where(n - 1 < qi, n - 1, n))

        def other_scores(n, qi=qi, lhs=lhs, block=block):
            kj = block(n)
            return jnp.dot(lhs[jnp.where(kj < qi, 0, 1)], kp_ref[kj], preferred_element_type=F32)

        def pair(t, m, accumulate=accumulate, block=block, other_scores=other_scores):
            s_ref[1] = other_scores(2 * t + 1)
            m = accumulate(s_ref[0], block(2 * t), m)
            s_ref[0] = other_scores(2 * t + 2)
            return accumulate(s_ref[1], block(2 * t + 1), m)

        m = lax.fori_loop(0, nk // 2 - 1, pair, jnp.full((2 * tq, 1), -jnp.inf, F32), unroll=True)
        s_ref[1] = other_scores(nk - 1)
        m = accumulate(s_ref[0], block(nk - 2), m)
        if j + 1 < qps:
            s_ref[0] = diagonal_scores(q_ref[0, (j + 1) * tq:(j + 2) * tq, :], qi + 1)
        else:
            s_ref[0] = diagonal_scores(qn_ref[0], jnp.minimum(qi + 1, nq - 1))
        accumulate(s_ref[1], block(nk - 1), m)

        a1, a2 = acc[0:tq], acc[tq:2 * tq]
        o = a1[:, 0:LANES] / a1[:, LANES:2 * LANES] - lam * (a2[:, 0:LANES] / a2[:, LANES:2 * LANES])
        y = o * lax.rsqrt(jnp.mean(o * o, axis=-1, keepdims=True) + EPS)
        o_ref[0, rows_j, :] = (y * gn_ref[...]) * (1.0 - lambda_init)


def diff_attn(qkv, lam_params, g_norm, lambda_init, tq, qps):
    b, t, _ = qkv.shape
    nq = t // tq
    ns = nq // qps
    kaug, qaug, dtab = _alibi_constants(t, tq)
    h_ = DIFF_HEADS
    return pl.pallas_call(
        functools.partial(_diff_kernel, tq=tq, nq=nq, qps=qps, lambda_init=lambda_init),
        out_shape=jax.ShapeDtypeStruct((b, t, DIFF_V), F32),
        grid=(b, h_, ns),
        in_specs=[pl.BlockSpec((1, qps * tq, LANES), lambda bi, h, si: (bi, si, h)),
                  pl.BlockSpec((1, tq, LANES), lambda bi, h, si: (bi, jnp.minimum((si + 1) * qps, nq - 1), h)),
                  pl.BlockSpec((1, t, LANES), lambda bi, h, si: (bi, 0, h_ + h)),
                  pl.BlockSpec((1, t, LANES), lambda bi, h, si: (bi, 0, 2 * h_ + h)),
                  pl.BlockSpec((1, nq, LANES, tq), lambda bi, h, si: (h, 0, 0, 0)),
                  pl.BlockSpec((1, qps * tq, LANES), lambda bi, h, si: (h, si, 0)),
                  pl.BlockSpec((1, 2 * tq, tq), lambda bi, h, si: (h, 0, 0)),
                  pl.BlockSpec((4, DIFF_HD), lambda bi, h, si: (0, 0)),
                  pl.BlockSpec((1, DIFF_DV), lambda bi, h, si: (0, 0))],
        out_specs=pl.BlockSpec((1, qps * tq, LANES), lambda bi, h, si: (bi, si, h)),
        scratch_shapes=[pltpu.VMEM((nq, 2 * LANES, tq), BF16),
                        pltpu.VMEM((t, 2 * LANES), BF16),
                        pltpu.VMEM((qps, 2, 2 * tq, 2 * LANES), BF16),
                        pltpu.VMEM((qps, 2 * tq, 2 * LANES), F32),
                        pltpu.VMEM((2, 2 * tq, tq), F32)],
        compiler_params=_cparams(("parallel", "parallel", "arbitrary")),
        name="diff_attn",
    )(qkv, qkv, qkv, qkv, kaug, qaug, dtab, lam_params.astype(F32), g_norm.reshape(1, DIFF_DV).astype(F32))


def _cross_kernel(q_ref, kv_ref, o_ref):
    for h in range(CROSS_HEADS):
        qh = q_ref[0, :, h * CROSS_HD:(h + 1) * CROSS_HD]
        kh = kv_ref[0, :, h * CROSS_HD:(h + 1) * CROSS_HD]
        vh = kv_ref[0, :, CROSS_Q + h * CROSS_HD:CROSS_Q + (h + 1) * CROSS_HD]
        s = lax.dot_general(qh, kh, _NT, preferred_element_type=F32) * (CROSS_HD ** -0.5)
        p = jnp.exp(s - jnp.max(s, axis=-1, keepdims=True))
        l = jnp.sum(p, axis=-1, keepdims=True)
        o = jnp.dot(p.astype(BF16), vh, preferred_element_type=F32)
        o_ref[0, :, h * CROSS_HD:(h + 1) * CROSS_HD] = o / l


def cross_attn(q, kv, tq):
    b, t, _ = q.shape
    return pl.pallas_call(
        _cross_kernel,
        out_shape=jax.ShapeDtypeStruct((b, t, CROSS_Q), F32),
        grid=(b, t // tq),
        in_specs=[pl.BlockSpec((1, tq, CROSS_Q), lambda bi, i: (bi, i, 0)),
                  pl.BlockSpec((1, N_MEM, 2 * CROSS_Q), lambda bi, i: (bi, 0, 0))],
        out_specs=pl.BlockSpec((1, tq, CROSS_Q), lambda bi, i: (bi, i, 0)),
        compiler_params=_cparams(("parallel", "parallel")),
        name="cross_attn",
    )(q, kv)


def _silu(z):
    return z / (1.0 + jnp.exp(-z))


def _sigmoid(z):
    return 1.0 / (1.0 + jnp.exp(-z))


def _merge_kernel(x_ref, of_ref, ob_ref, od_ref, oc_ref, ng_ref, wg_ref, gn_ref, wb_ref, wo_ref, fg_ref, o_ref,
                  *, final):
    x = x_ref[...]
    h = ((x * lax.rsqrt(jnp.mean(x * x, axis=-1, keepdims=True) + EPS)) * ng_ref[...]).astype(BF16)
    og = of_ref[...] + ob_ref[...]
    parts = []
    for hd in range(GLA_HEADS):
        oh = og[:, hd * GLA_DV:(hd + 1) * GLA_DV]
        parts.append(oh * lax.rsqrt(jnp.mean(oh * oh, axis=-1, keepdims=True) + EPS) * gn_ref[...])
    branches = (jnp.concatenate(parts, axis=1), od_ref[...], oc_ref[...])
    merged = None
    for i in range(N_BRANCH):
        z = jnp.dot(h, wg_ref[:, i * BRANCH_W:(i + 1) * BRANCH_W], preferred_element_type=F32)
        a = (branches[i] * _silu(z)).astype(BF16)
        y = jnp.dot(a, wb_ref[i], preferred_element_type=F32)
        c0 = N_BRANCH * BRANCH_W + i * D_MODEL
        gate = _sigmoid(jnp.dot(h, wg_ref[:, c0:c0 + D_MODEL], preferred_element_type=F32))
        merged = gate * y if merged is None else merged + gate * y
    x = x + jnp.dot(merged.astype(BF16), wo_ref[...], preferred_element_type=F32)
    if final:
        x = (x * lax.rsqrt(jnp.mean(x * x, axis=-1, keepdims=True) + EPS)) * fg_ref[...]
    o_ref[...] = x


def merge(x2d, of, ob, od, oc, norm_g, w_gates, gla_norm_g, wb, wo, final_g, final, tm):
    m, d = x2d.shape
    row = lambda w: pl.BlockSpec((tm, w), lambda i: (i, 0))
    return pl.pallas_call(
        functools.partial(_merge_kernel, final=final),
        out_shape=jax.ShapeDtypeStruct((m, d), F32),
        grid=(m // tm,),
        in_specs=[row(d), row(BRANCH_W), row(BRANCH_W), row(BRANCH_W), row(BRANCH_W),
                  _resident((1, d)), _resident(w_gates.shape), _resident((1, GLA_DV)),
                  _resident(wb.shape), _resident(wo.shape), _resident((1, d))],
        out_specs=row(d),
        compiler_params=_cparams(("parallel",)),
        name="merge_final" if final else "merge",
    )(x2d, of, ob, od, oc, norm_g.reshape(1, d).astype(F32), w_gates,
      gla_norm_g.reshape(1, GLA_DV).astype(F32), wb, wo, final_g.reshape(1, d).astype(F32))


def _split_w_in(w_in_l):
    offs = np.concatenate([[0], np.cumsum(np.array(SPLITS))])
    col = lambda i: w_in_l[:, int(offs[i]):int(offs[i + 1])]
    g_q, g_k, g_v, g_lr, z_a, d_q, d_k, d_v, z_b, c_q, z_c, gate_in = (col(i) for i in range(len(SPLITS)))
    lr_pad = jnp.pad(g_lr, ((0, 0), (0, GLR_PAD - 2 * GLA_RANK)))
    w_branch_in = jnp.concatenate([g_q, g_k, g_v, lr_pad, d_q, d_k, d_v, c_q], axis=1).astype(BF16)
    w_gates = jnp.concatenate([z_a, z_b, z_c, gate_in], axis=1).astype(BF16)
    return w_branch_in, w_gates


def _gate_weights(w2_l, b2_l, direction):
    w2pad = jnp.zeros((GLR_PAD, GLA_Q), F32)
    w2pad = w2pad.at[direction * GLA_RANK:(direction + 1) * GLA_RANK].set(w2_l[direction].astype(F32))
    return w2pad.astype(BF16), b2_l[direction].reshape(1, GLA_Q).astype(F32)


def _tiles(t):
    tq = min(t // 2, 512)
    qps = t // tq if t // tq <= 4 else 1
    return dict(tm=512, tb=min(t, 1024), tq=tq, qps=qps, tc=min(t, 512))


def _trunk(x, mem, norm_g, w_in, gla_gate_w2, gla_gate_b, gla_norm_g, diff_lambda, diff_norm_g,
           mem_norm_g, w_mem_kv, w_branch, w_out, final_norm_g):
    b, t, d = x.shape
    tl = _tiles(t)
    x2d = x.reshape(b * t, d)
    mem2d = mem.reshape(b * N_MEM, d)
    for l in range(DEPTH):
        lambda_init = 0.8 - 0.6 * math.exp(-0.3 * l)
        w_branch_in, w_gates = _split_w_in(w_in[l])
        g_in, qkv, cq = branch_inputs(x2d, norm_g[l], w_branch_in, tl["tm"])
        g_in, qkv, cq = g_in.reshape(b, t, GLA_IN), qkv.reshape(b, t, 3 * DIFF_QK), cq.reshape(b, t, CROSS_Q)
        kv = norm_matmul(mem2d, mem_norm_g[l], w_mem_kv[l].astype(BF16), BF16, N_MEM, 2 * CROSS_Q)
        kv = kv.reshape(b, N_MEM, 2 * CROSS_Q)
        o_f = gla(g_in, *_gate_weights(gla_gate_w2[l], gla_gate_b[l], 0), False, tl["tb"])
        o_b = gla(g_in, *_gate_weights(gla_gate_w2[l], gla_gate_b[l], 1), True, tl["tb"])
        o_d = diff_attn(qkv, diff_lambda[l], diff_norm_g[l], lambda_init, tl["tq"], tl["qps"])
        o_c = cross_attn(cq, kv, tl["tc"])
        x2d = merge(x2d, o_f.reshape(b * t, GLA_V), o_b.reshape(b * t, GLA_V), o_d.reshape(b * t, DIFF_V),
                    o_c.reshape(b * t, CROSS_Q), norm_g[l], w_gates, gla_norm_g[l], w_branch[l].astype(BF16),
                    w_out[l].astype(BF16), final_norm_g, l == DEPTH - 1, tl["tm"])
    return x2d.reshape(b, t, d)


def kernel(x_prompt, x_sample, mem_prompt, mem_sample, norm_g, w_in, gla_gate_w2, gla_gate_b, gla_norm_g,
           diff_lambda, diff_norm_g, mem_norm_g, w_mem_kv, w_branch, w_out, final_norm_g):
    params = (norm_g, w_in, gla_gate_w2, gla_gate_b, gla_norm_g, diff_lambda, diff_norm_g, mem_norm_g,
              w_mem_kv, w_branch, w_out, final_norm_g)
    return (_trunk(x_prompt, mem_prompt, *params), _trunk(x_sample, mem_sample, *params))
```

```python
import functools
import math

import numpy as np
import jax
import jax.numpy as jnp
from jax import lax
from jax.experimental import pallas as pl
from jax.experimental.pallas import tpu as pltpu

F32 = jnp.float32
BF16 = jnp.bfloat16

D_MODEL = 1024
DEPTH = 4
N_MEM = 256
BRANCH_W = 512
N_BRANCH = 3
EPS = 1e-6
GLA_HEADS = 4
GLA_DK = 64
GLA_DV = 128
GLA_RANK = 16
GLA_TAU = 16.0
GLA_Q = GLA_HEADS * GLA_DK
GLA_V = GLA_HEADS * GLA_DV
DIFF_HEADS = 4
DIFF_HD = 64
DIFF_DV = 2 * DIFF_HD
DIFF_QK = DIFF_HEADS * 2 * DIFF_HD
DIFF_V = DIFF_HEADS * DIFF_DV
CROSS_HEADS = 4
CROSS_HD = 128
CROSS_Q = CROSS_HEADS * CROSS_HD
SPLITS = (GLA_Q, GLA_Q, GLA_V, 2 * GLA_RANK, BRANCH_W,
          DIFF_QK, DIFF_QK, DIFF_V, BRANCH_W,
          CROSS_Q, BRANCH_W, N_BRANCH * D_MODEL)

LANES = 128
LOG2E = 1.4426950408889634
VMEM_LIMIT = 56 * 1024 * 1024
GLA_CHUNK = 128
GLA_LEVELS = 7
GLR_PAD = LANES
GLA_IN = 2 * GLA_Q + GLA_V + GLR_PAD

_NT = (((1,), (1,)), ((), ()))
_TN = (((0,), (0,)), ((), ()))


def _cparams(sem):
    return pltpu.CompilerParams(dimension_semantics=sem, vmem_limit_bytes=VMEM_LIMIT)


def _resident(shape):
    return pl.BlockSpec(shape, lambda *_: (0,) * len(shape), pipeline_mode=pl.Buffered(1))


def _norm_matmul_kernel(x_ref, g_ref, w_ref, o_ref, h_ref):
    @pl.when(pl.program_id(1) == 0)
    def _():
        x = x_ref[...]
        y = x * lax.rsqrt(jnp.mean(x * x, axis=-1, keepdims=True) + EPS)
        h_ref[...] = (y * g_ref[...]).astype(BF16)

    o_ref[...] = jnp.dot(h_ref[...], w_ref[...], preferred_element_type=F32).astype(o_ref.dtype)


def norm_matmul(x2d, g, w_bf16, out_dtype, tm, tn):
    m, d = x2d.shape
    n = w_bf16.shape[1]
    return pl.pallas_call(
        _norm_matmul_kernel,
        out_shape=jax.ShapeDtypeStruct((m, n), out_dtype),
        grid=(m // tm, n // tn),
        in_specs=[pl.BlockSpec((tm, d), lambda i, j: (i, 0)),
                  pl.BlockSpec((1, d), lambda i, j: (0, 0)),
                  pl.BlockSpec((d, tn), lambda i, j: (0, j))],
        out_specs=pl.BlockSpec((tm, tn), lambda i, j: (i, j)),
        scratch_shapes=[pltpu.VMEM((tm, d), BF16)],
        compiler_params=_cparams(("parallel", "arbitrary")),
        name="norm_matmul",
    )(x2d, g.reshape(1, d).astype(F32), w_bf16)


def _proj_kernel(x_ref, g_ref, w_ref, gla_ref, qkv_ref, cq_ref):
    x = x_ref[...]
    h = ((x * lax.rsqrt(jnp.mean(x * x, axis=-1, keepdims=True) + EPS)) * g_ref[...]).astype(BF16)
    c1, c2 = GLA_IN, GLA_IN + 3 * DIFF_QK
    gla_ref[...] = jnp.dot(h, w_ref[:, 0:c1], preferred_element_type=F32)
    qkv_ref[...] = jnp.dot(h, w_ref[:, c1:c2], preferred_element_type=F32).astype(BF16)
    cq_ref[...] = jnp.dot(h, w_ref[:, c2:c2 + CROSS_Q], preferred_element_type=F32).astype(BF16)


def branch_inputs(x2d, g, w_bf16, tm):
    m, d = x2d.shape
    row = lambda w: pl.BlockSpec((tm, w), lambda i: (i, 0))
    return pl.pallas_call(
        _proj_kernel,
        out_shape=(jax.ShapeDtypeStruct((m, GLA_IN), F32),
                   jax.ShapeDtypeStruct((m, 3 * DIFF_QK), BF16),
                   jax.ShapeDtypeStruct((m, CROSS_Q), BF16)),
        grid=(m // tm,),
        in_specs=[row(d), _resident((1, d)), _resident(w_bf16.shape)],
        out_specs=(row(GLA_IN), row(3 * DIFF_QK), row(CROSS_Q)),
        compiler_params=_cparams(("parallel",)),
        name="branch_inputs",
    )(x2d, g.reshape(1, d).astype(F32), w_bf16)


def _gla_constants(reverse):
    c = GLA_CHUNK
    t = np.arange(c)[:, None]
    u = np.arange(c)[None, :]
    start, end, total = (u <= t), (u > t), np.ones((c, c), bool)
    levels, masks = [], [(t == u)]
    for lvl in range(1, GLA_LEVELS + 1):
        bs, half = 1 << lvl, 1 << (lvl - 1)
        boundary = (t // bs) * bs + half - 1
        right = (t % bs) >= half
        levels.append(np.where(right, (u > boundary) & (u <= t), (u > t) & (u <= boundary)))
        masks.append(((u // bs) == (t // bs)) & right & ((u % bs) < half))
    flip = (lambda m: m[::-1, ::-1]) if reverse else (lambda m: m)
    mall_t = np.concatenate([flip(m).T for m in [start, end] + levels + [total]], axis=1).astype(np.float32)
    mask = np.stack([flip(m) for m in masks], axis=0).astype(np.float32)
    mask = np.concatenate([mask, mask], axis=-1)
    return jnp.asarray(mall_t, BF16), jnp.asarray(np.ascontiguousarray(mask), F32)


def _log_decay(logits):
    return (jnp.minimum(logits, 0.0) - jnp.log1p(jnp.exp(-jnp.abs(logits)))) * (1.0 / GLA_TAU)


def _gla_kernel(g_ref, mt_ref, mask_ref, w2t_ref, b2t_ref, o_ref, st_ref, *, reverse, n_chunks):
    c = GLA_CHUNK

    @pl.when(pl.program_id(1) == 0)
    def _():
        st_ref[...] = jnp.zeros_like(st_ref)

    lane_v = lax.broadcasted_iota(jnp.int32, (c, 2 * GLA_DV), 1)
    vlo, vhi = lane_v < GLA_DV, lane_v >= GLA_DV
    drow = lax.broadcasted_iota(jnp.int32, (LANES, c), 0)
    dlo, dhi = drow < GLA_DK, drow >= GLA_DK
    state_mask = ((lax.broadcasted_iota(jnp.int32, (LANES, 2 * GLA_DV), 0) < GLA_DK)
                  == (lax.broadcasted_iota(jnp.int32, (LANES, 2 * GLA_DV), 1) < GLA_DV))

    def chunk(ci, carry):
        idx = (n_chunks - 1 - ci) if reverse else ci
        rows = pl.ds(pl.multiple_of(idx * c, c), c)
        glr_t = g_ref[0, rows, 2 * GLA_Q + GLA_V:GLA_IN].T.astype(BF16)
        log_at = _log_decay(jnp.dot(w2t_ref[...], glr_t, preferred_element_type=F32) + b2t_ref[...])
        e_all = jnp.exp(jnp.dot(log_at.astype(BF16), mt_ref[...], preferred_element_type=F32))
        for p in range(GLA_HEADS // 2):
            q_t = g_ref[0, rows, p * LANES:(p + 1) * LANES].T * (GLA_DK ** -0.5)
            k_t = g_ref[0, rows, GLA_Q + p * LANES:GLA_Q + (p + 1) * LANES].T
            v = g_ref[0, rows, 2 * GLA_Q + p * 2 * GLA_DV:2 * GLA_Q + (p + 1) * 2 * GLA_DV]
            v2 = jnp.concatenate([jnp.where(vlo, v, 0.0), jnp.where(vhi, v, 0.0)], axis=0).astype(BF16)
            e = lambda blk: e_all[p * LANES:(p + 1) * LANES, blk * c:(blk + 1) * c]
            st = st_ref[p]
            o = lax.dot_general((q_t * e(0)).astype(BF16), st.astype(BF16), _TN, preferred_element_type=F32)
            att = jnp.zeros((c, 2 * c), F32)
            for lvl in range(GLA_LEVELS + 1):
                qs, ks = (q_t, k_t) if lvl == 0 else (q_t * e(lvl + 1), k_t * e(lvl + 1))
                k2 = jnp.concatenate([jnp.where(dlo, ks, 0.0), jnp.where(dhi, ks, 0.0)], axis=1)
                a = lax.dot_general(qs.astype(BF16), k2.astype(BF16), _TN, preferred_element_type=F32)
                att = att + a * mask_ref[lvl]
            o = o + jnp.dot(att.astype(BF16), v2, preferred_element_type=F32)
            o_ref[0, rows, p * 2 * GLA_DV:(p + 1) * 2 * GLA_DV] = o
            upd = jnp.dot((k_t * e(1)).astype(BF16), v.astype(BF16), preferred_element_type=F32)
            decay = e(GLA_LEVELS + 2)
            st_ref[p] = st * jnp.concatenate([decay, decay], axis=1) + jnp.where(state_mask, upd, 0.0)
        return carry

    lax.fori_loop(0, n_chunks, chunk, 0, unroll=True)


def gla(g_in, w2pad, b2, reverse, tb):
    b, t, _ = g_in.shape
    nblk = t // tb
    mall_t, mask = _gla_constants(reverse)
    b2t = jnp.broadcast_to(b2.reshape(GLA_Q, 1), (GLA_Q, GLA_CHUNK))
    tmap = (lambda bi, i: (bi, nblk - 1 - i, 0)) if reverse else (lambda bi, i: (bi, i, 0))
    consts = (mall_t, mask, w2pad.T, b2t)
    return pl.pallas_call(
        functools.partial(_gla_kernel, reverse=reverse, n_chunks=tb // GLA_CHUNK),
        out_shape=jax.ShapeDtypeStruct((b, t, GLA_V), F32),
        grid=(b, nblk),
        in_specs=[pl.BlockSpec((1, tb, GLA_IN), tmap)] + [_resident(a.shape) for a in consts],
        out_specs=pl.BlockSpec((1, tb, GLA_V), tmap),
        scratch_shapes=[pltpu.VMEM((GLA_HEADS // 2, LANES, 2 * GLA_DV), F32)],
        compiler_params=_cparams(("parallel", "arbitrary")),
        name="gla_bwd" if reverse else "gla_fwd",
    )(g_in, *consts)


def _bf16_terms(x, n=3):
    terms, rest = [], np.asarray(x, np.float64)
    for _ in range(n):
        bits = rest.astype(np.float32).view(np.uint32)
        bits = (bits + np.uint32(0x7FFF) + ((bits >> np.uint32(16)) & np.uint32(1))) & np.uint32(0xFFFF0000)
        term = bits.view(np.float32)
        terms.append(term)
        rest = rest - term.astype(np.float64)
    return terms


def _alibi_constants(t, tq):
    slopes = LOG2E * 2.0 ** (-8.0 * (np.arange(DIFF_HEADS, dtype=np.float64) + 1.0) / DIFF_HEADS)
    pos = np.arange(t)
    parts = ((pos // LANES) * LANES, pos % LANES)
    n = 3 * len(parts)
    kaug = np.zeros((DIFF_HEADS, t, LANES), np.float32)
    qaug = np.zeros((DIFF_HEADS, t, LANES), np.float32)
    kaug[:, :, n:2 * n] = 1.0
    qaug[:, :, 0:n] = 1.0
    for h in range(DIFF_HEADS):
        for j, part in enumerate(parts):
            for i, term in enumerate(_bf16_terms(slopes[h] * part)):
                kaug[h, :, 3 * j + i] = -term
                qaug[h, :, n + 3 * j + i] = term
    i = np.arange(tq)
    dtab = -slopes[:, None, None] * np.abs(i[:, None] - i[None, :])[None]
    dtab = np.concatenate([dtab, dtab], axis=1)
    kaug_t = kaug.reshape(DIFF_HEADS, t // tq, tq, LANES).transpose(0, 1, 3, 2)
    return jnp.asarray(kaug_t, BF16), jnp.asarray(qaug, BF16), jnp.asarray(dtab, F32)


def _diff_kernel(q_ref, qn_ref, k_ref, v_ref, kaug_ref, qaug_ref, dtab_ref, lp_ref, gn_ref, o_ref,
                 kp_ref, vp_ref, lhs_ref, acc_ref, s_ref, *, tq, nq, qps, lambda_init):
    step = pl.program_id(2)
    nk = nq
    assert nk % 2 == 0
    lane = lax.broadcasted_iota(jnp.int32, (tq, LANES), 1)

    def stacked_q(q):
        qs = q * (LOG2E * DIFF_HD ** -0.5)
        return jnp.concatenate([jnp.where(lane < DIFF_HD, qs, 0.0), jnp.where(lane >= DIFF_HD, qs, 0.0)],
                               axis=0).astype(BF16)

    def key_rows(kj):
        return pl.ds(pl.multiple_of(kj * tq, tq), tq)

    def diagonal_scores(q, kj):
        lhs = jnp.concatenate([stacked_q(q), jnp.zeros((2 * tq, LANES), BF16)], axis=1)
        return jnp.dot(lhs, kp_ref[kj], preferred_element_type=F32) + dtab_ref[0]

    def new_head():
        for kb in range(nk):
            kp_ref[kb, 0:LANES, :] = k_ref[0, kb * tq:(kb + 1) * tq, :].T
            kp_ref[kb, LANES:2 * LANES, :] = kaug_ref[0, kb]
        vp_ref[:, 0:LANES] = v_ref[0]
        vp_ref[:, LANES:2 * LANES] = jnp.ones((vp_ref.shape[0], LANES), BF16)
        s_ref[0] = diagonal_scores(q_ref[0, 0:tq, :], 0)

    if nq == qps:
        new_head()
    else:
        pl.when(step == 0)(new_head)

    lp = lp_ref[...]
    lam = (jnp.exp(jnp.sum(lp[0:1] * lp[1:2], axis=-1, keepdims=True))
           - jnp.exp(jnp.sum(lp[2:3] * lp[3:4], axis=-1, keepdims=True)) + lambda_init)

    for j in range(qps):
        qi = step * qps + j
        rows_j = slice(j * tq, (j + 1) * tq)
        lhs, acc = lhs_ref.at[j], acc_ref.at[j]
        qa = jnp.concatenate([qaug_ref[0, rows_j, :], qaug_ref[0, rows_j, :]], axis=0)
        lhs[:, :, 0:LANES] = jnp.broadcast_to(stacked_q(q_ref[0, rows_j, :])[None], (2, 2 * tq, LANES))
        lhs[0, :, LANES:2 * LANES] = -qa
        lhs[1, :, LANES:2 * LANES] = qa
        acc[...] = jnp.zeros((2 * tq, 2 * LANES), F32)

        def accumulate(s, kj, m, acc=acc):
            m_new = jnp.maximum(m, jnp.max(s, axis=-1, keepdims=True))
            p = jnp.exp2(s - m_new)
            acc[...] = acc[...] * jnp.exp2(m - m_new) + jnp.dot(p.astype(BF16), vp_ref[key_rows(kj), :],
                                                                preferred_element_type=F32)
            return m_new

        def block(n, qi=qi):
            return jnp.where(n == 0, qi, jnp.where(n - 1 < qi, n - 1, n))

        def other_scores(n, qi=qi, lhs=lhs, block=block):
            kj = block(n)
            return jnp.dot(lhs[jnp.where(kj < qi, 0, 1)], kp_ref[kj], preferred_element_type=F32)

        def pair(t, m, accumulate=accumulate, block=block, other_scores=other_scores):
            s_ref[1] = other_scores(2 * t + 1)
            m = accumulate(s_ref[0], block(2 * t), m)
            s_ref[0] = other_scores(2 * t + 2)
            return accumulate(s_ref[1], block(2 * t + 1), m)

        m = lax.fori_loop(0, nk // 2 - 1, pair, jnp.full((2 * tq, 1), -jnp.inf, F32), unroll=True)
        s_ref[1] = other_scores(nk - 1)
        m = accumulate(s_ref[0], block(nk - 2), m)
        if j + 1 < qps:
            s_ref[0] = diagonal_scores(q_ref[0, (j + 1) * tq:(j + 2) * tq, :], qi + 1)
        else:
            s_ref[0] = diagonal_scores(qn_ref[0], jnp.minimum(qi + 1, nq - 1))
        accumulate(s_ref[1], block(nk - 1), m)

        a1, a2 = acc[0:tq], acc[tq:2 * tq]
        o = a1[:, 0:LANES] / a1[:, LANES:2 * LANES] - lam * (a2[:, 0:LANES] / a2[:, LANES:2 * LANES])
        y = o * lax.rsqrt(jnp.mean(o * o, axis=-1, keepdims=True) + EPS)
        o_ref[0, rows_j, :] = (y * gn_ref[...]) * (1.0 - lambda_init)


def diff_attn(qkv, lam_params, g_norm, lambda_init, tq, qps):
    b, t, _ = qkv.shape
    nq = t // tq
    ns = nq // qps
    kaug, qaug, dtab = _alibi_constants(t, tq)
    h_ = DIFF_HEADS
    return pl.pallas_call(
        functools.partial(_diff_kernel, tq=tq, nq=nq, qps=qps, lambda_init=lambda_init),
        out_shape=jax.ShapeDtypeStruct((b, t, DIFF_V), F32),
        grid=(b, h_, ns),
        in_specs=[pl.BlockSpec((1, qps * tq, LANES), lambda bi, h, si: (bi, si, h)),
                  pl.BlockSpec((1, tq, LANES), lambda bi, h, si: (bi, jnp.minimum((si + 1) * qps, nq - 1), h)),
                  pl.BlockSpec((1, t, LANES), lambda bi, h, si: (bi, 0, h_ + h)),
                  pl.BlockSpec((1, t, LANES), lambda bi, h, si: (bi, 0, 2 * h_ + h)),
                  pl.BlockSpec((1, nq, LANES, tq), lambda bi, h, si: (h, 0, 0, 0)),
                  pl.BlockSpec((1, qps * tq, LANES), lambda bi, h, si: (h, si, 0)),
                  pl.BlockSpec((1, 2 * tq, tq), lambda bi, h, si: (h, 0, 0)),
                  pl.BlockSpec((4, DIFF_HD), lambda bi, h, si: (0, 0)),
                  pl.BlockSpec((1, DIFF_DV), lambda bi, h, si: (0, 0))],
        out_specs=pl.BlockSpec((1, qps * tq, LANES), lambda bi, h, si: (bi, si, h)),
        scratch_shapes=[pltpu.VMEM((nq, 2 * LANES, tq), BF16),
                        pltpu.VMEM((t, 2 * LANES), BF16),
                        pltpu.VMEM((qps, 2, 2 * tq, 2 * LANES), BF16),
                        pltpu.VMEM((qps, 2 * tq, 2 * LANES), F32),
                        pltpu.VMEM((2, 2 * tq, tq), F32)],
        compiler_params=_cparams(("parallel", "parallel", "arbitrary")),
        name="diff_attn",
    )(qkv, qkv, qkv, qkv, kaug, qaug, dtab, lam_params.astype(F32), g_norm.reshape(1, DIFF_DV).astype(F32))


def _cross_kernel(q_ref, kv_ref, o_ref):
    for h in range(CROSS_HEADS):
        qh = q_ref[0, :, h * CROSS_HD:(h + 1) * CROSS_HD]
        kh = kv_ref[0, :, h * CROSS_HD:(h + 1) * CROSS_HD]
        vh = kv_ref[0, :, CROSS_Q + h * CROSS_HD:CROSS_Q + (h + 1) * CROSS_HD]
        s = lax.dot_general(qh, kh, _NT, preferred_element_type=F32) * (CROSS_HD ** -0.5)
        p = jnp.exp(s - jnp.max(s, axis=-1, keepdims=True))
        l = jnp.sum(p, axis=-1, keepdims=True)
        o = jnp.dot(p.astype(BF16), vh, preferred_element_type=F32)
        o_ref[0, :, h * CROSS_HD:(h + 1) * CROSS_HD] = o / l


def cross_attn(q, kv, tq):
    b, t, _ = q.shape
    return pl.pallas_call(
        _cross_kernel,
        out_shape=jax.ShapeDtypeStruct((b, t, CROSS_Q), F32),
        grid=(b, t // tq),
        in_specs=[pl.BlockSpec((1, tq, CROSS_Q), lambda bi, i: (bi, i, 0)),
                  pl.BlockSpec((1, N_MEM, 2 * CROSS_Q), lambda bi, i: (bi, 0, 0))],
        out_specs=pl.BlockSpec((1, tq, CROSS_Q), lambda bi, i: (bi, i, 0)),
        compiler_params=_cparams(("parallel", "parallel")),
        name="cross_attn",
    )(q, kv)


def _silu(z):
    return z / (1.0 + jnp.exp(-z))


def _sigmoid(z):
    return 1.0 / (1.0 + jnp.exp(-z))


def _merge_kernel(x_ref, of_ref, ob_ref, od_ref, oc_ref, ng_ref, wg_ref, gn_ref, wb_ref, wo_ref, fg_ref, o_ref,
                  *, final):
    x = x_ref[...]
    h = ((x * lax.rsqrt(jnp.mean(x * x, axis=-1, keepdims=True) + EPS)) * ng_ref[...]).astype(BF16)
    og = of_ref[...] + ob_ref[...]
    parts = []
    for hd in range(GLA_HEADS):
        oh = og[:, hd * GLA_DV:(hd + 1) * GLA_DV]
        parts.append(oh * lax.rsqrt(jnp.mean(oh * oh, axis=-1, keepdims=True) + EPS) * gn_ref[...])
    branches = (jnp.concatenate(parts, axis=1), od_ref[...], oc_ref[...])
    merged = None
    for i in range(N_BRANCH):
        z = jnp.dot(h, wg_ref[:, i * BRANCH_W:(i + 1) * BRANCH_W], preferred_element_type=F32)
        a = (branches[i] * _silu(z)).astype(BF16)
        y = jnp.dot(a, wb_ref[i], preferred_element_type=F32)
        c0 = N_BRANCH * BRANCH_W + i * D_MODEL
        gate = _sigmoid(jnp.dot(h, wg_ref[:, c0:c0 + D_MODEL], preferred_element_type=F32))
        merged = gate * y if merged is None else merged + gate * y
    x = x + jnp.dot(merged.astype(BF16), wo_ref[...], preferred_element_type=F32)
    if final:
        x = (x * lax.rsqrt(jnp.mean(x * x, axis=-1, keepdims=True) + EPS)) * fg_ref[...]
    o_ref[...] = x


def merge(x2d, of, ob, od, oc, norm_g, w_gates, gla_norm_g, wb, wo, final_g, final, tm):
    m, d = x2d.shape
    row = lambda w: pl.BlockSpec((tm, w), lambda i: (i, 0))
    return pl.pallas_call(
        functools.partial(_merge_kernel, final=final),
        out_shape=jax.ShapeDtypeStruct((m, d), F32),
        grid=(m // tm,),
        in_specs=[row(d), row(BRANCH_W), row(BRANCH_W), row(BRANCH_W), row(BRANCH_W),
                  _resident((1, d)), _resident(w_gates.shape), _resident((1, GLA_DV)),
                  _resident(wb.shape), _resident(wo.shape), _resident((1, d))],
        out_specs=row(d),
        compiler_params=_cparams(("parallel",)),
        name="merge_final" if final else "merge",
    )(x2d, of, ob, od, oc, norm_g.reshape(1, d).astype(F32), w_gates,
      gla_norm_g.reshape(1, GLA_DV).astype(F32), wb, wo, final_g.reshape(1, d).astype(F32))


def _split_w_in(w_in_l):
    offs = np.concatenate([[0], np.cumsum(np.array(SPLITS))])
    col = lambda i: w_in_l[:, int(offs[i]):int(offs[i + 1])]
    g_q, g_k, g_v, g_lr, z_a, d_q, d_k, d_v, z_b, c_q, z_c, gate_in = (col(i) for i in range(len(SPLITS)))
    lr_pad = jnp.pad(g_lr, ((0, 0), (0, GLR_PAD - 2 * GLA_RANK)))
    w_branch_in = jnp.concatenate([g_q, g_k, g_v, lr_pad, d_q, d_k, d_v, c_q], axis=1).astype(BF16)
    w_gates = jnp.concatenate([z_a, z_b, z_c, gate_in], axis=1).astype(BF16)
    return w_branch_in, w_gates


def _gate_weights(w2_l, b2_l, direction):
    w2pad = jnp.zeros((GLR_PAD, GLA_Q), F32)
    w2pad = w2pad.at[direction * GLA_RANK:(direction + 1) * GLA_RANK].set(w2_l[direction].astype(F32))
    return w2pad.astype(BF16), b2_l[direction].reshape(1, GLA_Q).astype(F32)


def _tiles(t):
    tq = min(t // 2, 512)
    qps = 2 if t // tq <= 4 else 1
    return dict(tm=512, tb=min(t, 1024), tq=tq, qps=qps, tc=min(t, 512))


def _trunk(x, mem, norm_g, w_in, gla_gate_w2, gla_gate_b, gla_norm_g, diff_lambda, diff_norm_g,
           mem_norm_g, w_mem_kv, w_branch, w_out, final_norm_g):
    b, t, d = x.shape
    tl = _tiles(t)
    x2d = x.reshape(b * t, d)
    mem2d = mem.reshape(b * N_MEM, d)
    for l in range(DEPTH):
        lambda_init = 0.8 - 0.6 * math.exp(-0.3 * l)
        w_branch_in, w_gates = _split_w_in(w_in[l])
        g_in, qkv, cq = branch_inputs(x2d, norm_g[l], w_branch_in, tl["tm"])
        g_in, qkv, cq = g_in.reshape(b, t, GLA_IN), qkv.reshape(b, t, 3 * DIFF_QK), cq.reshape(b, t, CROSS_Q)
        kv = norm_matmul(mem2d, mem_norm_g[l], w_mem_kv[l].astype(BF16), BF16, N_MEM, 2 * CROSS_Q)
        kv = kv.reshape(b, N_MEM, 2 * CROSS_Q)
        o_f = gla(g_in, *_gate_weights(gla_gate_w2[l], gla_gate_b[l], 0), False, tl["tb"])
        o_b = gla(g_in, *_gate_weights(gla_gate_w2[l], gla_gate_b[l], 1), True, tl["tb"])
        o_d = diff_attn(qkv, diff_lambda[l], diff_norm_g[l], lambda_init, tl["tq"], tl["qps"])
        o_c = cross_attn(cq, kv, tl["tc"])
        x2d = merge(x2d, o_f.reshape(b * t, GLA_V), o_b.reshape(b * t, GLA_V), o_d.reshape(b * t, DIFF_V),
                    o_c.reshape(b * t, CROSS_Q), norm_g[l], w_gates, gla_norm_g[l], w_branch[l].astype(BF16),
                    w_out[l].astype(BF16), final_norm_g, l == DEPTH - 1, tl["tm"])
    return x2d.reshape(b, t, d)


def kernel(x_prompt, x_sample, mem_prompt, mem_sample, norm_g, w_in, gla_gate_w2, gla_gate_b, gla_norm_g,
           diff_lambda, diff_norm_g, mem_norm_g, w_mem_kv, w_branch, w_out, final_norm_g):
    params = (norm_g, w_in, gla_gate_w2, gla_gate_b, gla_norm_g, diff_lambda, diff_norm_g, mem_norm_g,
              w_mem_kv, w_branch, w_out, final_norm_g)
    return (_trunk(x_prompt, mem_prompt, *params), _trunk(x_sample, mem_sample, *params))
```

```python
import functools
import math

import numpy as np
import jax
import jax.numpy as jnp
from jax import lax
from jax.experimental import pallas as pl
from jax.experimental.pallas import tpu as pltpu

F32 = jnp.float32
BF16 = jnp.bfloat16

D_MODEL = 1024
DEPTH = 4
N_MEM = 256
BRANCH_W = 512
N_BRANCH = 3
EPS = 1e-6
GLA_HEADS = 4
GLA_DK = 64
GLA_DV = 128
GLA_RANK = 16
GLA_TAU = 16.0
GLA_Q = GLA_HEADS * GLA_DK
GLA_V = GLA_HEADS * GLA_DV
DIFF_HEADS = 4
DIFF_HD = 64
DIFF_DV = 2 * DIFF_HD
DIFF_QK = DIFF_HEADS * 2 * DIFF_HD
DIFF_V = DIFF_HEADS * DIFF_DV
CROSS_HEADS = 4
CROSS_HD = 128
CROSS_Q = CROSS_HEADS * CROSS_HD
SPLITS = (GLA_Q, GLA_Q, GLA_V, 2 * GLA_RANK, BRANCH_W,
          DIFF_QK, DIFF_QK, DIFF_V, BRANCH_W,
          CROSS_Q, BRANCH_W, N_BRANCH * D_MODEL)

LANES = 128
LOG2E = 1.4426950408889634
VMEM_LIMIT = 56 * 1024 * 1024
GLA_CHUNK = 128
GLA_LEVELS = 7
GLR_PAD = LANES
GLA_IN = 2 * GLA_Q + GLA_V + GLR_PAD

_NT = (((1,), (1,)), ((), ()))
_TN = (((0,), (0,)), ((), ()))


def _cparams(sem):
    return pltpu.CompilerParams(dimension_semantics=sem, vmem_limit_bytes=VMEM_LIMIT)


def _resident(shape):
    return pl.BlockSpec(shape, lambda *_: (0,) * len(shape), pipeline_mode=pl.Buffered(1))


def _norm_matmul_kernel(x_ref, g_ref, w_ref, o_ref, h_ref):
    @pl.when(pl.program_id(1) == 0)
    def _():
        x = x_ref[...]
        y = x * lax.rsqrt(jnp.mean(x * x, axis=-1, keepdims=True) + EPS)
        h_ref[...] = (y * g_ref[...]).astype(BF16)

    o_ref[...] = jnp.dot(h_ref[...], w_ref[...], preferred_element_type=F32).astype(o_ref.dtype)


def norm_matmul(x2d, g, w_bf16, out_dtype, tm, tn):
    m, d = x2d.shape
    n = w_bf16.shape[1]
    return pl.pallas_call(
        _norm_matmul_kernel,
        out_shape=jax.ShapeDtypeStruct((m, n), out_dtype),
        grid=(m // tm, n // tn),
        in_specs=[pl.BlockSpec((tm, d), lambda i, j: (i, 0)),
                  pl.BlockSpec((1, d), lambda i, j: (0, 0)),
                  pl.BlockSpec((d, tn), lambda i, j: (0, j))],
        out_specs=pl.BlockSpec((tm, tn), lambda i, j: (i, j)),
        scratch_shapes=[pltpu.VMEM((tm, d), BF16)],
        compiler_params=_cparams(("parallel", "arbitrary")),
        name="norm_matmul",
    )(x2d, g.reshape(1, d).astype(F32), w_bf16)


def _proj_kernel(x_ref, g_ref, w_ref, gla_ref, qkv_ref, cq_ref):
    x = x_ref[...]
    h = ((x * lax.rsqrt(jnp.mean(x * x, axis=-1, keepdims=True) + EPS)) * g_ref[...]).astype(BF16)
    c1, c2 = GLA_IN, GLA_IN + 3 * DIFF_QK
    gla_ref[...] = jnp.dot(h, w_ref[:, 0:c1], preferred_element_type=F32)
    qkv_ref[...] = jnp.dot(h, w_ref[:, c1:c2], preferred_element_type=F32).astype(BF16)
    cq_ref[...] = jnp.dot(h, w_ref[:, c2:c2 + CROSS_Q], preferred_element_type=F32).astype(BF16)


def branch_inputs(x2d, g, w_bf16, tm):
    m, d = x2d.shape
    row = lambda w: pl.BlockSpec((tm, w), lambda i: (i, 0))
    return pl.pallas_call(
        _proj_kernel,
        out_shape=(jax.ShapeDtypeStruct((m, GLA_IN), F32),
                   jax.ShapeDtypeStruct((m, 3 * DIFF_QK), BF16),
                   jax.ShapeDtypeStruct((m, CROSS_Q), BF16)),
        grid=(m // tm,),
        in_specs=[row(d), _resident((1, d)), _resident(w_bf16.shape)],
        out_specs=(row(GLA_IN), row(3 * DIFF_QK), row(CROSS_Q)),
        compiler_params=_cparams(("parallel",)),
        name="branch_inputs",
    )(x2d, g.reshape(1, d).astype(F32), w_bf16)


def _gla_constants(reverse):
    c = GLA_CHUNK
    t = np.arange(c)[:, None]
    u = np.arange(c)[None, :]
    start, end, total = (u <= t), (u > t), np.ones((c, c), bool)
    levels, masks = [], [(t == u)]
    for lvl in range(1, GLA_LEVELS + 1):
        bs, half = 1 << lvl, 1 << (lvl - 1)
        boundary = (t // bs) * bs + half - 1
        right = (t % bs) >= half
        levels.append(np.where(right, (u > boundary) & (u <= t), (u > t) & (u <= boundary)))
        masks.append(((u // bs) == (t // bs)) & right & ((u % bs) < half))
    flip = (lambda m: m[::-1, ::-1]) if reverse else (lambda m: m)
    mall_t = np.concatenate([flip(m).T for m in [start, end] + levels + [total]], axis=1).astype(np.float32)
    mask = np.stack([flip(m) for m in masks], axis=0).astype(np.float32)
    mask = np.concatenate([mask, mask], axis=-1)
    return jnp.asarray(mall_t, BF16), jnp.asarray(np.ascontiguousarray(mask), F32)


def _log_decay(logits):
    return (jnp.minimum(logits, 0.0) - jnp.log1p(jnp.exp(-jnp.abs(logits)))) * (1.0 / GLA_TAU)


def _gla_kernel(g_ref, mt_ref, mask_ref, w2t_ref, b2t_ref, o_ref, st_ref, *, reverse, n_chunks):
    c = GLA_CHUNK

    @pl.when(pl.program_id(1) == 0)
    def _():
        st_ref[...] = jnp.zeros_like(st_ref)

    lane_v = lax.broadcasted_iota(jnp.int32, (c, 2 * GLA_DV), 1)
    vlo, vhi = lane_v < GLA_DV, lane_v >= GLA_DV
    drow = lax.broadcasted_iota(jnp.int32, (LANES, c), 0)
    dlo, dhi = drow < GLA_DK, drow >= GLA_DK
    state_mask = ((lax.broadcasted_iota(jnp.int32, (LANES, 2 * GLA_DV), 0) < GLA_DK)
                  == (lax.broadcasted_iota(jnp.int32, (LANES, 2 * GLA_DV), 1) < GLA_DV))

    def chunk(ci, carry):
        idx = (n_chunks - 1 - ci) if reverse else ci
        rows = pl.ds(pl.multiple_of(idx * c, c), c)
        glr_t = g_ref[0, rows, 2 * GLA_Q + GLA_V:GLA_IN].T.astype(BF16)
        log_at = _log_decay(jnp.dot(w2t_ref[...], glr_t, preferred_element_type=F32) + b2t_ref[...])
        e_all = jnp.exp(jnp.dot(log_at.astype(BF16), mt_ref[...], preferred_element_type=F32))
        for p in range(GLA_HEADS // 2):
            q_t = g_ref[0, rows, p * LANES:(p + 1) * LANES].T * (GLA_DK ** -0.5)
            k_t = g_ref[0, rows, GLA_Q + p * LANES:GLA_Q + (p + 1) * LANES].T
            v = g_ref[0, rows, 2 * GLA_Q + p * 2 * GLA_DV:2 * GLA_Q + (p + 1) * 2 * GLA_DV]
            v2 = jnp.concatenate([jnp.where(vlo, v, 0.0), jnp.where(vhi, v, 0.0)], axis=0).astype(BF16)
            e = lambda blk: e_all[p * LANES:(p + 1) * LANES, blk * c:(blk + 1) * c]
            st = st_ref[p]
            o = lax.dot_general((q_t * e(0)).astype(BF16), st.astype(BF16), _TN, preferred_element_type=F32)
            att = jnp.zeros((c, 2 * c), F32)
            for lvl in range(GLA_LEVELS + 1):
                qs, ks = (q_t, k_t) if lvl == 0 else (q_t * e(lvl + 1), k_t * e(lvl + 1))
                k2 = jnp.concatenate([jnp.where(dlo, ks, 0.0), jnp.where(dhi, ks, 0.0)], axis=1)
                a = lax.dot_general(qs.astype(BF16), k2.astype(BF16), _TN, preferred_element_type=F32)
                att = att + a * mask_ref[lvl]
            o = o + jnp.dot(att.astype(BF16), v2, preferred_element_type=F32)
            o_ref[0, rows, p * 2 * GLA_DV:(p + 1) * 2 * GLA_DV] = o
            upd = jnp.dot((k_t * e(1)).astype(BF16), v.astype(BF16), preferred_element_type=F32)
            decay = e(GLA_LEVELS + 2)
            st_ref[p] = st * jnp.concatenate([decay, decay], axis=1) + jnp.where(state_mask, upd, 0.0)
        return carry

    lax.fori_loop(0, n_chunks, chunk, 0, unroll=True)


def gla(g_in, w2pad, b2, reverse, tb):
    b, t, _ = g_in.shape
    nblk = t // tb
    mall_t, mask = _gla_constants(reverse)
    b2t = jnp.broadcast_to(b2.reshape(GLA_Q, 1), (GLA_Q, GLA_CHUNK))
    tmap = (lambda bi, i: (bi, nblk - 1 - i, 0)) if reverse else (lambda bi, i: (bi, i, 0))
    consts = (mall_t, mask, w2pad.T, b2t)
    return pl.pallas_call(
        functools.partial(_gla_kernel, reverse=reverse, n_chunks=tb // GLA_CHUNK),
        out_shape=jax.ShapeDtypeStruct((b, t, GLA_V), F32),
        grid=(b, nblk),
        in_specs=[pl.BlockSpec((1, tb, GLA_IN), tmap)] + [_resident(a.shape) for a in consts],
        out_specs=pl.BlockSpec((1, tb, GLA_V), tmap),
        scratch_shapes=[pltpu.VMEM((GLA_HEADS // 2, LANES, 2 * GLA_DV), F32)],
        compiler_params=_cparams(("parallel", "arbitrary")),
        name="gla_bwd" if reverse else "gla_fwd",
    )(g_in, *consts)


def _bf16_terms(x, n=3):
    terms, rest = [], np.asarray(x, np.float64)
    for _ in range(n):
        bits = rest.astype(np.float32).view(np.uint32)
        bits = (bits + np.uint32(0x7FFF) + ((bits >> np.uint32(16)) & np.uint32(1))) & np.uint32(0xFFFF0000)
        term = bits.view(np.float32)
        terms.append(term)
        rest = rest - term.astype(np.float64)
    return terms


def _alibi_constants(t, tq):
    slopes = LOG2E * 2.0 ** (-8.0 * (np.arange(DIFF_HEADS, dtype=np.float64) + 1.0) / DIFF_HEADS)
    pos = np.arange(t)
    parts = ((pos // LANES) * LANES, pos % LANES)
    n = 3 * len(parts)
    kaug = np.zeros((DIFF_HEADS, t, LANES), np.float32)
    qaug = np.zeros((DIFF_HEADS, t, LANES), np.float32)
    kaug[:, :, n:2 * n] = 1.0
    qaug[:, :, 0:n] = 1.0
    for h in range(DIFF_HEADS):
        for j, part in enumerate(parts):
            for i, term in enumerate(_bf16_terms(slopes[h] * part)):
                kaug[h, :, 3 * j + i] = -term
                qaug[h, :, n + 3 * j + i] = term
    i = np.arange(tq)
    dtab = -slopes[:, None, None] * np.abs(i[:, None] - i[None, :])[None]
    dtab = np.concatenate([dtab, dtab], axis=1)
    kaug_t = kaug.reshape(DIFF_HEADS, t // tq, tq, LANES).transpose(0, 1, 3, 2)
    slope_rows = np.broadcast_to(slopes[:, None, None], (DIFF_HEADS, 1, LANES))
    return jnp.asarray(kaug_t, BF16), jnp.asarray(qaug, BF16), jnp.asarray(dtab, F32), jnp.asarray(slope_rows, F32)


def _diff_kernel(q_ref, qn_ref, qf_ref, k_ref, v_ref, kaug_ref, qaug_ref, dtab_ref, slope_ref, lp_ref, gn_ref,
                 o_ref, kp_ref, vp_ref, lhs_ref, acc_ref, s_ref, reach_ref, *, tq, nq, qps, banded, lambda_init):
    step = pl.program_id(2)
    nk = nq
    assert nk % 2 == 0
    lane = lax.broadcasted_iota(jnp.int32, (tq, LANES), 1)

    def stacked_q(q):
        qs = q * (LOG2E * DIFF_HD ** -0.5)
        return jnp.concatenate([jnp.where(lane < DIFF_HD, qs, 0.0), jnp.where(lane >= DIFF_HD, qs, 0.0)],
                               axis=0).astype(BF16)

    def key_rows(kj):
        return pl.ds(pl.multiple_of(kj * tq, tq), tq)

    def diagonal_scores(q, kj):
        lhs = jnp.concatenate([stacked_q(q), jnp.zeros((2 * tq, LANES), BF16)], axis=1)
        return jnp.dot(lhs, kp_ref[kj], preferred_element_type=F32) + dtab_ref[0]

    def largest_half_norm_sq(x):
        half = (lax.broadcasted_iota(jnp.int32, (LANES, LANES), 0) < DIFF_HD) == (
            lax.broadcasted_iota(jnp.int32, (LANES, LANES), 1) < DIFF_HD)
        xf = x.astype(F32)
        sq = jnp.dot((xf * xf).astype(BF16), jnp.where(half, 1.0, 0.0).astype(BF16), preferred_element_type=F32)
        return 1.02 * jnp.max(jnp.max(sq, axis=0, keepdims=True), axis=1, keepdims=True)

    def new_head():
        for kb in range(nk):
            kp_ref[kb, 0:LANES, :] = k_ref[0, kb * tq:(kb + 1) * tq, :].T
            kp_ref[kb, LANES:2 * LANES, :] = kaug_ref[0, kb]
        vp_ref[:, 0:LANES] = v_ref[0]
        vp_ref[:, LANES:2 * LANES] = jnp.ones((vp_ref.shape[0], LANES), BF16)
        s_ref[0] = diagonal_scores(q_ref[0, 0:tq, :], 0)
        if banded:
            bound = jnp.sqrt(largest_half_norm_sq(qf_ref[0]) * largest_half_norm_sq(k_ref[0])) * (
                1.02 * LOG2E * DIFF_HD ** -0.5)
            blocks = ((150.0 + 2.0 * bound) / slope_ref[0, :, 0:1] - 1.0) * (1.0 / tq)
            reach = jnp.where(blocks < nk, jnp.floor(blocks) + 1.0, float(nk))
            reach_ref[0] = reach.astype(jnp.int32)[0, 0]

    if nq == qps:
        new_head()
    else:
        pl.when(step == 0)(new_head)

    lp = lp_ref[...]
    lam = (jnp.exp(jnp.sum(lp[0:1] * lp[1:2], axis=-1, keepdims=True))
           - jnp.exp(jnp.sum(lp[2:3] * lp[3:4], axis=-1, keepdims=True)) + lambda_init)

    for j in range(qps):
        qi = step * qps + j
        rows_j = slice(j * tq, (j + 1) * tq)
        lhs, acc = lhs_ref.at[j], acc_ref.at[j]
        qa = jnp.concatenate([qaug_ref[0, rows_j, :], qaug_ref[0, rows_j, :]], axis=0)
        lhs[:, :, 0:LANES] = jnp.broadcast_to(stacked_q(q_ref[0, rows_j, :])[None], (2, 2 * tq, LANES))
        lhs[0, :, LANES:2 * LANES] = -qa
        lhs[1, :, LANES:2 * LANES] = qa
        acc[...] = jnp.zeros((2 * tq, 2 * LANES), F32)

        if banded:
            reach = reach_ref[0]
            lo, hi = jnp.maximum(qi - reach, 0), jnp.minimum(qi + reach, nk - 1)
            odd = (hi - lo) % 2 == 0
            grow_hi = odd & (hi < nk - 1)
            lo, hi = jnp.where(odd & ~grow_hi, lo - 1, lo), jnp.where(grow_hi, hi + 1, hi)
            count = hi - lo + 1
        else:
            lo, count = 0, nk

        def accumulate(s, kj, m, acc=acc):
            m_new = jnp.maximum(m, jnp.max(s, axis=-1, keepdims=True))
            p = jnp.exp2(s - m_new)
            acc[...] = acc[...] * jnp.exp2(m - m_new) + jnp.dot(p.astype(BF16), vp_ref[key_rows(kj), :],
                                                                preferred_element_type=F32)
            return m_new

        def block(n, qi=qi, lo=lo):
            return jnp.where(n == 0, qi, jnp.where(lo + n - 1 < qi, lo + n - 1, lo + n))

        def other_scores(n, qi=qi, lhs=lhs, block=block):
            kj = block(n)
            return jnp.dot(lhs[jnp.where(kj < qi, 0, 1)], kp_ref[kj], preferred_element_type=F32)

        def pair(t, m, accumulate=accumulate, block=block, other_scores=other_scores):
            s_ref[1] = other_scores(2 * t + 1)
            m = accumulate(s_ref[0], block(2 * t), m)
            s_ref[0] = other_scores(2 * t + 2)
            return accumulate(s_ref[1], block(2 * t + 1), m)

        m0 = jnp.full((2 * tq, 1), -jnp.inf, F32)
        if banded:
            m = lax.fori_loop(0, count // 2 - 1, pair, m0)
        else:
            m = lax.fori_loop(0, count // 2 - 1, pair, m0, unroll=True)
        s_ref[1] = other_scores(count - 1)
        m = accumulate(s_ref[0], block(count - 2), m)
        if j + 1 < qps:
            s_ref[0] = diagonal_scores(q_ref[0, (j + 1) * tq:(j + 2) * tq, :], qi + 1)
        else:
            s_ref[0] = diagonal_scores(qn_ref[0], jnp.minimum(qi + 1, nq - 1))
        accumulate(s_ref[1], block(count - 1), m)

        a1, a2 = acc[0:tq], acc[tq:2 * tq]
        o = a1[:, 0:LANES] / a1[:, LANES:2 * LANES] - lam * (a2[:, 0:LANES] / a2[:, LANES:2 * LANES])
        y = o * lax.rsqrt(jnp.mean(o * o, axis=-1, keepdims=True) + EPS)
        o_ref[0, rows_j, :] = (y * gn_ref[...]) * (1.0 - lambda_init)


def diff_attn(qkv, lam_params, g_norm, lambda_init, tq, qps, heads=(0, DIFF_HEADS), banded=False):
    b, t, _ = qkv.shape
    nq = t // tq
    ns = nq // qps
    kaug, qaug, dtab, slope = _alibi_constants(t, tq)
    h_ = DIFF_HEADS
    h0, nh = heads[0], heads[1] - heads[0]
    return pl.pallas_call(
        functools.partial(_diff_kernel, tq=tq, nq=nq, qps=qps, banded=banded, lambda_init=lambda_init),
        out_shape=jax.ShapeDtypeStruct((b, t, nh * DIFF_DV), F32),
        grid=(b, nh, ns),
        in_specs=[pl.BlockSpec((1, qps * tq, LANES), lambda bi, h, si: (bi, si, h0 + h)),
                  pl.BlockSpec((1, tq, LANES), lambda bi, h, si: (bi, jnp.minimum((si + 1) * qps, nq - 1), h0 + h)),
                  pl.BlockSpec((1, t, LANES), lambda bi, h, si: (bi, 0, h0 + h)),
                  pl.BlockSpec((1, t, LANES), lambda bi, h, si: (bi, 0, h_ + h0 + h)),
                  pl.BlockSpec((1, t, LANES), lambda bi, h, si: (bi, 0, 2 * h_ + h0 + h)),
                  pl.BlockSpec((1, nq, LANES, tq), lambda bi, h, si: (h0 + h, 0, 0, 0)),
                  pl.BlockSpec((1, qps * tq, LANES), lambda bi, h, si: (h0 + h, si, 0)),
                  pl.BlockSpec((1, 2 * tq, tq), lambda bi, h, si: (h0 + h, 0, 0)),
                  pl.BlockSpec((1, 1, LANES), lambda bi, h, si: (h0 + h, 0, 0)),
                  pl.BlockSpec((4, DIFF_HD), lambda bi, h, si: (0, 0)),
                  pl.BlockSpec((1, DIFF_DV), lambda bi, h, si: (0, 0))],
        out_specs=pl.BlockSpec((1, qps * tq, LANES), lambda bi, h, si: (bi, si, h)),
        scratch_shapes=[pltpu.VMEM((nq, 2 * LANES, tq), BF16),
                        pltpu.VMEM((t, 2 * LANES), BF16),
                        pltpu.VMEM((qps, 2, 2 * tq, 2 * LANES), BF16),
                        pltpu.VMEM((qps, 2 * tq, 2 * LANES), F32),
                        pltpu.VMEM((2, 2 * tq, tq), F32),
                        pltpu.SMEM((1,), jnp.int32)],
        compiler_params=_cparams(("parallel", "parallel", "arbitrary")),
        name="diff_attn_banded" if banded else "diff_attn",
    )(qkv, qkv, qkv, qkv, qkv, kaug, qaug, dtab, slope, lam_params.astype(F32),
      g_norm.reshape(1, DIFF_DV).astype(F32))


def _cross_kernel(q_ref, kv_ref, o_ref):
    for h in range(CROSS_HEADS):
        qh = q_ref[0, :, h * CROSS_HD:(h + 1) * CROSS_HD]
        kh = kv_ref[0, :, h * CROSS_HD:(h + 1) * CROSS_HD]
        vh = kv_ref[0, :, CROSS_Q + h * CROSS_HD:CROSS_Q + (h + 1) * CROSS_HD]
        s = lax.dot_general(qh, kh, _NT, preferred_element_type=F32) * (CROSS_HD ** -0.5)
        p = jnp.exp(s - jnp.max(s, axis=-1, keepdims=True))
        l = jnp.sum(p, axis=-1, keepdims=True)
        o = jnp.dot(p.astype(BF16), vh, preferred_element_type=F32)
        o_ref[0, :, h * CROSS_HD:(h + 1) * CROSS_HD] = o / l


def cross_attn(q, kv, tq):
    b, t, _ = q.shape
    return pl.pallas_call(
        _cross_kernel,
        out_shape=jax.ShapeDtypeStruct((b, t, CROSS_Q), F32),
        grid=(b, t // tq),
        in_specs=[pl.BlockSpec((1, tq, CROSS_Q), lambda bi, i: (bi, i, 0)),
                  pl.BlockSpec((1, N_MEM, 2 * CROSS_Q), lambda bi, i: (bi, 0, 0))],
        out_specs=pl.BlockSpec((1, tq, CROSS_Q), lambda bi, i: (bi, i, 0)),
        compiler_params=_cparams(("parallel", "parallel")),
        name="cross_attn",
    )(q, kv)


def _silu(z):
    return z / (1.0 + jnp.exp(-z))


def _sigmoid(z):
    return 1.0 / (1.0 + jnp.exp(-z))


def _merge_kernel(x_ref, of_ref, ob_ref, od_ref, oc_ref, ng_ref, wg_ref, gn_ref, wb_ref, wo_ref, fg_ref, o_ref,
                  *, final):
    x = x_ref[...]
    h = ((x * lax.rsqrt(jnp.mean(x * x, axis=-1, keepdims=True) + EPS)) * ng_ref[...]).astype(BF16)
    og = of_ref[...] + ob_ref[...]
    parts = []
    for hd in range(GLA_HEADS):
        oh = og[:, hd * GLA_DV:(hd + 1) * GLA_DV]
        parts.append(oh * lax.rsqrt(jnp.mean(oh * oh, axis=-1, keepdims=True) + EPS) * gn_ref[...])
    branches = (jnp.concatenate(parts, axis=1), od_ref[...], oc_ref[...])
    merged = None
    for i in range(N_BRANCH):
        z = jnp.dot(h, wg_ref[:, i * BRANCH_W:(i + 1) * BRANCH_W], preferred_element_type=F32)
        a = (branches[i] * _silu(z)).astype(BF16)
        y = jnp.dot(a, wb_ref[i], preferred_element_type=F32)
        c0 = N_BRANCH * BRANCH_W + i * D_MODEL
        gate = _sigmoid(jnp.dot(h, wg_ref[:, c0:c0 + D_MODEL], preferred_element_type=F32))
        merged = gate * y if merged is None else merged + gate * y
    x = x + jnp.dot(merged.astype(BF16), wo_ref[...], preferred_element_type=F32)
    if final:
        x = (x * lax.rsqrt(jnp.mean(x * x, axis=-1, keepdims=True) + EPS)) * fg_ref[...]
    o_ref[...] = x


def merge(x2d, of, ob, od, oc, norm_g, w_gates, gla_norm_g, wb, wo, final_g, final, tm):
    m, d = x2d.shape
    row = lambda w: pl.BlockSpec((tm, w), lambda i: (i, 0))
    return pl.pallas_call(
        functools.partial(_merge_kernel, final=final),
        out_shape=jax.ShapeDtypeStruct((m, d), F32),
        grid=(m // tm,),
        in_specs=[row(d), row(BRANCH_W), row(BRANCH_W), row(BRANCH_W), row(BRANCH_W),
                  _resident((1, d)), _resident(w_gates.shape), _resident((1, GLA_DV)),
                  _resident(wb.shape), _resident(wo.shape), _resident((1, d))],
        out_specs=row(d),
        compiler_params=_cparams(("parallel",)),
        name="merge_final" if final else "merge",
    )(x2d, of, ob, od, oc, norm_g.reshape(1, d).astype(F32), w_gates,
      gla_norm_g.reshape(1, GLA_DV).astype(F32), wb, wo, final_g.reshape(1, d).astype(F32))


def _split_w_in(w_in_l):
    offs = np.concatenate([[0], np.cumsum(np.array(SPLITS))])
    col = lambda i: w_in_l[:, int(offs[i]):int(offs[i + 1])]
    g_q, g_k, g_v, g_lr, z_a, d_q, d_k, d_v, z_b, c_q, z_c, gate_in = (col(i) for i in range(len(SPLITS)))
    lr_pad = jnp.pad(g_lr, ((0, 0), (0, GLR_PAD - 2 * GLA_RANK)))
    w_branch_in = jnp.concatenate([g_q, g_k, g_v, lr_pad, d_q, d_k, d_v, c_q], axis=1).astype(BF16)
    w_gates = jnp.concatenate([z_a, z_b, z_c, gate_in], axis=1).astype(BF16)
    return w_branch_in, w_gates


def _gate_weights(w2_l, b2_l, direction):
    w2pad = jnp.zeros((GLR_PAD, GLA_Q), F32)
    w2pad = w2pad.at[direction * GLA_RANK:(direction + 1) * GLA_RANK].set(w2_l[direction].astype(F32))
    return w2pad.astype(BF16), b2_l[direction].reshape(1, GLA_Q).astype(F32)


def _tiles(t):
    tq = min(t // 2, 512)
    qps = 2 if t // tq <= 4 else 1
    band_heads = DIFF_HEADS // 2 if t // tq > 4 else 0
    return dict(tm=512, tb=min(t, 1024), tq=tq, qps=qps, band_heads=band_heads, tc=min(t, 512))


def _trunk(x, mem, norm_g, w_in, gla_gate_w2, gla_gate_b, gla_norm_g, diff_lambda, diff_norm_g,
           mem_norm_g, w_mem_kv, w_branch, w_out, final_norm_g):
    b, t, d = x.shape
    tl = _tiles(t)
    x2d = x.reshape(b * t, d)
    mem2d = mem.reshape(b * N_MEM, d)
    for l in range(DEPTH):
        lambda_init = 0.8 - 0.6 * math.exp(-0.3 * l)
        w_branch_in, w_gates = _split_w_in(w_in[l])
        g_in, qkv, cq = branch_inputs(x2d, norm_g[l], w_branch_in, tl["tm"])
        g_in, qkv, cq = g_in.reshape(b, t, GLA_IN), qkv.reshape(b, t, 3 * DIFF_QK), cq.reshape(b, t, CROSS_Q)
        kv = norm_matmul(mem2d, mem_norm_g[l], w_mem_kv[l].astype(BF16), BF16, N_MEM, 2 * CROSS_Q)
        kv = kv.reshape(b, N_MEM, 2 * CROSS_Q)
        o_f = gla(g_in, *_gate_weights(gla_gate_w2[l], gla_gate_b[l], 0), False, tl["tb"])
        o_b = gla(g_in, *_gate_weights(gla_gate_w2[l], gla_gate_b[l], 1), True, tl["tb"])
        if tl["band_heads"]:
            hb = tl["band_heads"]
            o_d = jnp.concatenate(
                [diff_attn(qkv, diff_lambda[l], diff_norm_g[l], lambda_init, tl["tq"], 1, (0, hb), True),
                 diff_attn(qkv, diff_lambda[l], diff_norm_g[l], lambda_init, tl["tq"], 1, (hb, DIFF_HEADS))], axis=-1)
        else:
            o_d = diff_attn(qkv, diff_lambda[l], diff_norm_g[l], lambda_init, tl["tq"], tl["qps"])
        o_c = cross_attn(cq, kv, tl["tc"])
        x2d = merge(x2d, o_f.reshape(b * t, GLA_V), o_b.reshape(b * t, GLA_V), o_d.reshape(b * t, DIFF_V),
                    o_c.reshape(b * t, CROSS_Q), norm_g[l], w_gates, gla_norm_g[l], w_branch[l].astype(BF16),
                    w_out[l].astype(BF16), final_norm_g, l == DEPTH - 1, tl["tm"])
    return x2d.reshape(b, t, d)


def kernel(x_prompt, x_sample, mem_prompt, mem_sample, norm_g, w_in, gla_gate_w2, gla_gate_b, gla_norm_g,
           diff_lambda, diff_norm_g, mem_norm_g, w_mem_kv, w_branch, w_out, final_norm_g):
    params = (norm_g, w_in, gla_gate_w2, gla_gate_b, gla_norm_g, diff_lambda, diff_norm_g, mem_norm_g,
              w_mem_kv, w_branch, w_out, final_norm_g)
    return (_trunk(x_prompt, mem_prompt, *params), _trunk(x_sample, mem_sample, *params))
```

```python
import functools
import math

import numpy as np
import jax
import jax.numpy as jnp
from jax import lax
from jax.experimental import pallas as pl
from jax.experimental.pallas import tpu as pltpu

F32 = jnp.float32
BF16 = jnp.bfloat16

D_MODEL = 1024
DEPTH = 4
N_MEM = 256
BRANCH_W = 512
N_BRANCH = 3
EPS = 1e-6
GLA_HEADS = 4
GLA_DK = 64
GLA_DV = 128
GLA_RANK = 16
GLA_TAU = 16.0
GLA_Q = GLA_HEADS * GLA_DK
GLA_V = GLA_HEADS * GLA_DV
DIFF_HEADS = 4
DIFF_HD = 64
DIFF_DV = 2 * DIFF_HD
DIFF_QK = DIFF_HEADS * 2 * DIFF_HD
DIFF_V = DIFF_HEADS * DIFF_DV
CROSS_HEADS = 4
CROSS_HD = 128
CROSS_Q = CROSS_HEADS * CROSS_HD
SPLITS = (GLA_Q, GLA_Q, GLA_V, 2 * GLA_RANK, BRANCH_W,
          DIFF_QK, DIFF_QK, DIFF_V, BRANCH_W,
          CROSS_Q, BRANCH_W, N_BRANCH * D_MODEL)

LANES = 128
LOG2E = 1.4426950408889634
VMEM_LIMIT = 56 * 1024 * 1024
GLA_CHUNK = 128
GLA_LEVELS = 7
GLR_PAD = LANES
GLA_IN = 2 * GLA_Q + GLA_V + GLR_PAD

_NT = (((1,), (1,)), ((), ()))
_TN = (((0,), (0,)), ((), ()))


def _cparams(sem):
    return pltpu.CompilerParams(dimension_semantics=sem, vmem_limit_bytes=VMEM_LIMIT)


def _resident(shape):
    return pl.BlockSpec(shape, lambda *_: (0,) * len(shape), pipeline_mode=pl.Buffered(1))


def _norm_matmul_kernel(x_ref, g_ref, w_ref, o_ref, h_ref):
    @pl.when(pl.program_id(1) == 0)
    def _():
        x = x_ref[...]
        y = x * lax.rsqrt(jnp.mean(x * x, axis=-1, keepdims=True) + EPS)
        h_ref[...] = (y * g_ref[...]).astype(BF16)

    o_ref[...] = jnp.dot(h_ref[...], w_ref[...], preferred_element_type=F32).astype(o_ref.dtype)


def norm_matmul(x2d, g, w_bf16, out_dtype, tm, tn):
    m, d = x2d.shape
    n = w_bf16.shape[1]
    return pl.pallas_call(
        _norm_matmul_kernel,
        out_shape=jax.ShapeDtypeStruct((m, n), out_dtype),
        grid=(m // tm, n // tn),
        in_specs=[pl.BlockSpec((tm, d), lambda i, j: (i, 0)),
                  pl.BlockSpec((1, d), lambda i, j: (0, 0)),
                  pl.BlockSpec((d, tn), lambda i, j: (0, j))],
        out_specs=pl.BlockSpec((tm, tn), lambda i, j: (i, j)),
        scratch_shapes=[pltpu.VMEM((tm, d), BF16)],
        compiler_params=_cparams(("parallel", "arbitrary")),
        name="norm_matmul",
    )(x2d, g.reshape(1, d).astype(F32), w_bf16)


def _proj_kernel(x_ref, g_ref, w_ref, gla_ref, qkv_ref, cq_ref):
    x = x_ref[...]
    h = ((x * lax.rsqrt(jnp.mean(x * x, axis=-1, keepdims=True) + EPS)) * g_ref[...]).astype(BF16)
    c1, c2 = GLA_IN, GLA_IN + 3 * DIFF_QK
    gla_ref[...] = jnp.dot(h, w_ref[:, 0:c1], preferred_element_type=F32)
    qkv_ref[...] = jnp.dot(h, w_ref[:, c1:c2], preferred_element_type=F32).astype(BF16)
    cq_ref[...] = jnp.dot(h, w_ref[:, c2:c2 + CROSS_Q], preferred_element_type=F32).astype(BF16)


def branch_inputs(x2d, g, w_bf16, tm):
    m, d = x2d.shape
    row = lambda w: pl.BlockSpec((tm, w), lambda i: (i, 0))
    return pl.pallas_call(
        _proj_kernel,
        out_shape=(jax.ShapeDtypeStruct((m, GLA_IN), F32),
                   jax.ShapeDtypeStruct((m, 3 * DIFF_QK), BF16),
                   jax.ShapeDtypeStruct((m, CROSS_Q), BF16)),
        grid=(m // tm,),
        in_specs=[row(d), _resident((1, d)), _resident(w_bf16.shape)],
        out_specs=(row(GLA_IN), row(3 * DIFF_QK), row(CROSS_Q)),
        compiler_params=_cparams(("parallel",)),
        name="branch_inputs",
    )(x2d, g.reshape(1, d).astype(F32), w_bf16)


def _gla_constants(reverse):
    c = GLA_CHUNK
    t = np.arange(c)[:, None]
    u = np.arange(c)[None, :]
    start, end, total = (u <= t), (u > t), np.ones((c, c), bool)
    levels, masks = [], [(t == u)]
    for lvl in range(1, GLA_LEVELS + 1):
        bs, half = 1 << lvl, 1 << (lvl - 1)
        boundary = (t // bs) * bs + half - 1
        right = (t % bs) >= half
        levels.append(np.where(right, (u > boundary) & (u <= t), (u > t) & (u <= boundary)))
        masks.append(((u // bs) == (t // bs)) & right & ((u % bs) < half))
    flip = (lambda m: m[::-1, ::-1]) if reverse else (lambda m: m)
    mall_t = np.concatenate([flip(m).T for m in [start, end] + levels + [total]], axis=1).astype(np.float32)
    mask = np.stack([flip(m) for m in masks], axis=0).astype(np.float32)
    mask = np.concatenate([mask, mask], axis=-1)
    return jnp.asarray(mall_t, BF16), jnp.asarray(np.ascontiguousarray(mask), F32)


def _log_decay(logits):
    return (jnp.minimum(logits, 0.0) - jnp.log1p(jnp.exp(-jnp.abs(logits)))) * (1.0 / GLA_TAU)


def _gla_kernel(g_ref, mt_ref, mask_ref, w2t_ref, b2t_ref, o_ref, st_ref, *, reverse, n_chunks):
    c = GLA_CHUNK

    @pl.when(pl.program_id(1) == 0)
    def _():
        st_ref[...] = jnp.zeros_like(st_ref)

    lane_v = lax.broadcasted_iota(jnp.int32, (c, 2 * GLA_DV), 1)
    vlo, vhi = lane_v < GLA_DV, lane_v >= GLA_DV
    drow = lax.broadcasted_iota(jnp.int32, (LANES, c), 0)
    dlo, dhi = drow < GLA_DK, drow >= GLA_DK
    state_mask = ((lax.broadcasted_iota(jnp.int32, (LANES, 2 * GLA_DV), 0) < GLA_DK)
                  == (lax.broadcasted_iota(jnp.int32, (LANES, 2 * GLA_DV), 1) < GLA_DV))

    def chunk(ci, carry):
        idx = (n_chunks - 1 - ci) if reverse else ci
        rows = pl.ds(pl.multiple_of(idx * c, c), c)
        glr_t = g_ref[0, rows, 2 * GLA_Q + GLA_V:GLA_IN].T.astype(BF16)
        log_at = _log_decay(jnp.dot(w2t_ref[...], glr_t, preferred_element_type=F32) + b2t_ref[...])
        e_all = jnp.exp(jnp.dot(log_at.astype(BF16), mt_ref[...], preferred_element_type=F32))
        for p in range(GLA_HEADS // 2):
            q_t = g_ref[0, rows, p * LANES:(p + 1) * LANES].T * (GLA_DK ** -0.5)
            k_t = g_ref[0, rows, GLA_Q + p * LANES:GLA_Q + (p + 1) * LANES].T
            v = g_ref[0, rows, 2 * GLA_Q + p * 2 * GLA_DV:2 * GLA_Q + (p + 1) * 2 * GLA_DV]
            v2 = jnp.concatenate([jnp.where(vlo, v, 0.0), jnp.where(vhi, v, 0.0)], axis=0).astype(BF16)
            e = lambda blk: e_all[p * LANES:(p + 1) * LANES, blk * c:(blk + 1) * c]
            st = st_ref[p]
            o = lax.dot_general((q_t * e(0)).astype(BF16), st.astype(BF16), _TN, preferred_element_type=F32)
            att = jnp.zeros((c, 2 * c), F32)
            for lvl in range(GLA_LEVELS + 1):
                qs, ks = (q_t, k_t) if lvl == 0 else (q_t * e(lvl + 1), k_t * e(lvl + 1))
                k2 = jnp.concatenate([jnp.where(dlo, ks, 0.0), jnp.where(dhi, ks, 0.0)], axis=1)
                a = lax.dot_general(qs.astype(BF16), k2.astype(BF16), _TN, preferred_element_type=F32)
                att = att + a * mask_ref[lvl]
            o = o + jnp.dot(att.astype(BF16), v2, preferred_element_type=F32)
            o_ref[0, rows, p * 2 * GLA_DV:(p + 1) * 2 * GLA_DV] = o
            upd = jnp.dot((k_t * e(1)).astype(BF16), v.astype(BF16), preferred_element_type=F32)
            decay = e(GLA_LEVELS + 2)
            st_ref[p] = st * jnp.concatenate([decay, decay], axis=1) + jnp.where(state_mask, upd, 0.0)
        return carry

    lax.fori_loop(0, n_chunks, chunk, 0, unroll=True)


def gla(g_in, w2pad, b2, reverse, tb):
    b, t, _ = g_in.shape
    nblk = t // tb
    mall_t, mask = _gla_constants(reverse)
    b2t = jnp.broadcast_to(b2.reshape(GLA_Q, 1), (GLA_Q, GLA_CHUNK))
    tmap = (lambda bi, i: (bi, nblk - 1 - i, 0)) if reverse else (lambda bi, i: (bi, i, 0))
    consts = (mall_t, mask, w2pad.T, b2t)
    return pl.pallas_call(
        functools.partial(_gla_kernel, reverse=reverse, n_chunks=tb // GLA_CHUNK),
        out_shape=jax.ShapeDtypeStruct((b, t, GLA_V), F32),
        grid=(b, nblk),
        in_specs=[pl.BlockSpec((1, tb, GLA_IN), tmap)] + [_resident(a.shape) for a in consts],
        out_specs=pl.BlockSpec((1, tb, GLA_V), tmap),
        scratch_shapes=[pltpu.VMEM((GLA_HEADS // 2, LANES, 2 * GLA_DV), F32)],
        compiler_params=_cparams(("parallel", "arbitrary")),
        name="gla_bwd" if reverse else "gla_fwd",
    )(g_in, *consts)


def _bf16_terms(x, n=3):
    terms, rest = [], np.asarray(x, np.float64)
    for _ in range(n):
        bits = rest.astype(np.float32).view(np.uint32)
        bits = (bits + np.uint32(0x7FFF) + ((bits >> np.uint32(16)) & np.uint32(1))) & np.uint32(0xFFFF0000)
        term = bits.view(np.float32)
        terms.append(term)
        rest = rest - term.astype(np.float64)
    return terms


def _alibi_constants(t, tq):
    slopes = LOG2E * 2.0 ** (-8.0 * (np.arange(DIFF_HEADS, dtype=np.float64) + 1.0) / DIFF_HEADS)
    pos = np.arange(t)
    parts = ((pos // LANES) * LANES, pos % LANES)
    n = 3 * len(parts)
    kaug = np.zeros((DIFF_HEADS, t, LANES), np.float32)
    qaug = np.zeros((DIFF_HEADS, t, LANES), np.float32)
    kaug[:, :, n:2 * n] = 1.0
    qaug[:, :, 0:n] = 1.0
    for h in range(DIFF_HEADS):
        for j, part in enumerate(parts):
            for i, term in enumerate(_bf16_terms(slopes[h] * part)):
                kaug[h, :, 3 * j + i] = -term
                qaug[h, :, n + 3 * j + i] = term
    i = np.arange(tq)
    dtab = -slopes[:, None, None] * np.abs(i[:, None] - i[None, :])[None]
    dtab = np.concatenate([dtab, dtab], axis=1)
    kaug_t = kaug.reshape(DIFF_HEADS, t // tq, tq, LANES).transpose(0, 1, 3, 2)
    slope_rows = np.broadcast_to(slopes[:, None, None], (DIFF_HEADS, 1, LANES))
    return jnp.asarray(kaug_t, BF16), jnp.asarray(qaug, BF16), jnp.asarray(dtab, F32), jnp.asarray(slope_rows, F32)


def _diff_kernel(q_ref, qn_ref, qf_ref, k_ref, v_ref, kaug_ref, qaug_ref, dtab_ref, slope_ref, lp_ref, gn_ref,
                 o_ref, kp_ref, vp_ref, lhs_ref, acc_ref, s_ref, reach_ref, *, tq, nq, qps, banded, lambda_init):
    step = pl.program_id(2)
    nk = nq
    assert nk % 2 == 0
    lane = lax.broadcasted_iota(jnp.int32, (tq, LANES), 1)

    def stacked_q(q):
        qs = q * (LOG2E * DIFF_HD ** -0.5)
        return jnp.concatenate([jnp.where(lane < DIFF_HD, qs, 0.0), jnp.where(lane >= DIFF_HD, qs, 0.0)],
                               axis=0).astype(BF16)

    def key_rows(kj):
        return pl.ds(pl.multiple_of(kj * tq, tq), tq)

    def diagonal_scores(q, kj):
        lhs = jnp.concatenate([stacked_q(q), jnp.zeros((2 * tq, LANES), BF16)], axis=1)
        return jnp.dot(lhs, kp_ref[kj], preferred_element_type=F32) + dtab_ref[0]

    def half_sums(x):
        half = (lax.broadcasted_iota(jnp.int32, (LANES, LANES), 0) < DIFF_HD) == (
            lax.broadcasted_iota(jnp.int32, (LANES, LANES), 1) < DIFF_HD)
        return jnp.dot(x.astype(BF16), jnp.where(half, 1.0, 0.0).astype(BF16), preferred_element_type=F32)

    def overall(reduce, x):
        return reduce(reduce(x, axis=0, keepdims=True), axis=1, keepdims=True)

    def reach_blocks():
        qs = (qf_ref[0] * (LOG2E * DIFF_HD ** -0.5)).astype(F32)
        ks = k_ref[0].astype(F32)
        bound = 1.02 * jnp.sqrt(overall(jnp.max, half_sums(qs * qs)) * overall(jnp.max, half_sums(ks * ks)))
        m_low = overall(jnp.min, half_sums(qs * ks) - 0.01 * half_sums(jnp.abs(qs * ks)) - 0.1)
        blocks = ((150.0 + bound - m_low) / slope_ref[0, :, 0:1] - 1.0) * (1.0 / tq)
        return jnp.where(blocks < nk, jnp.floor(blocks) + 1.0, float(nk)).astype(jnp.int32)[0, 0]

    def new_head():
        for kb in range(nk):
            kp_ref[kb, 0:LANES, :] = k_ref[0, kb * tq:(kb + 1) * tq, :].T
            kp_ref[kb, LANES:2 * LANES, :] = kaug_ref[0, kb]
        vp_ref[:, 0:LANES] = v_ref[0]
        vp_ref[:, LANES:2 * LANES] = jnp.ones((vp_ref.shape[0], LANES), BF16)
        s_ref[0] = diagonal_scores(q_ref[0, 0:tq, :], 0)
        if banded:
            reach_ref[0] = reach_blocks()

    if nq == qps:
        new_head()
    else:
        pl.when(step == 0)(new_head)

    lp = lp_ref[...]
    lam = (jnp.exp(jnp.sum(lp[0:1] * lp[1:2], axis=-1, keepdims=True))
           - jnp.exp(jnp.sum(lp[2:3] * lp[3:4], axis=-1, keepdims=True)) + lambda_init)

    for j in range(qps):
        qi = step * qps + j
        rows_j = slice(j * tq, (j + 1) * tq)
        lhs, acc = lhs_ref.at[j], acc_ref.at[j]
        qa = jnp.concatenate([qaug_ref[0, rows_j, :], qaug_ref[0, rows_j, :]], axis=0)
        lhs[:, :, 0:LANES] = jnp.broadcast_to(stacked_q(q_ref[0, rows_j, :])[None], (2, 2 * tq, LANES))
        lhs[0, :, LANES:2 * LANES] = -qa
        lhs[1, :, LANES:2 * LANES] = qa
        acc[...] = jnp.zeros((2 * tq, 2 * LANES), F32)

        if banded:
            reach = reach_ref[0]
            lo, hi = jnp.maximum(qi - reach, 0), jnp.minimum(qi + reach, nk - 1)
            odd = (hi - lo) % 2 == 0
            grow_hi = odd & (hi < nk - 1)
            lo, hi = jnp.where(odd & ~grow_hi, lo - 1, lo), jnp.where(grow_hi, hi + 1, hi)
            count = hi - lo + 1
        else:
            lo, count = 0, nk

        def accumulate(s, kj, m, acc=acc):
            m_new = jnp.maximum(m, jnp.max(s, axis=-1, keepdims=True))
            p = jnp.exp2(s - m_new)
            acc[...] = acc[...] * jnp.exp2(m - m_new) + jnp.dot(p.astype(BF16), vp_ref[key_rows(kj), :],
                                                                preferred_element_type=F32)
            return m_new

        def block(n, qi=qi, lo=lo):
            return jnp.where(n == 0, qi, jnp.where(lo + n - 1 < qi, lo + n - 1, lo + n))

        def other_scores(n, qi=qi, lhs=lhs, block=block):
            kj = block(n)
            return jnp.dot(lhs[jnp.where(kj < qi, 0, 1)], kp_ref[kj], preferred_element_type=F32)

        def pair(t, m, accumulate=accumulate, block=block, other_scores=other_scores):
            s_ref[1] = other_scores(2 * t + 1)
            m = accumulate(s_ref[0], block(2 * t), m)
            s_ref[0] = other_scores(2 * t + 2)
            return accumulate(s_ref[1], block(2 * t + 1), m)

        m0 = jnp.full((2 * tq, 1), -jnp.inf, F32)
        if banded:
            m = lax.fori_loop(0, count // 2 - 1, pair, m0)
        else:
            m = lax.fori_loop(0, count // 2 - 1, pair, m0, unroll=True)
        s_ref[1] = other_scores(count - 1)
        m = accumulate(s_ref[0], block(count - 2), m)
        if j + 1 < qps:
            s_ref[0] = diagonal_scores(q_ref[0, (j + 1) * tq:(j + 2) * tq, :], qi + 1)
        else:
            s_ref[0] = diagonal_scores(qn_ref[0], jnp.minimum(qi + 1, nq - 1))
        accumulate(s_ref[1], block(count - 1), m)

        a1, a2 = acc[0:tq], acc[tq:2 * tq]
        o = a1[:, 0:LANES] / a1[:, LANES:2 * LANES] - lam * (a2[:, 0:LANES] / a2[:, LANES:2 * LANES])
        y = o * lax.rsqrt(jnp.mean(o * o, axis=-1, keepdims=True) + EPS)
        o_ref[0, rows_j, :] = (y * gn_ref[...]) * (1.0 - lambda_init)


def diff_attn(qkv, lam_params, g_norm, lambda_init, tq, qps, heads=(0, DIFF_HEADS), banded=False):
    b, t, _ = qkv.shape
    nq = t // tq
    ns = nq // qps
    kaug, qaug, dtab, slope = _alibi_constants(t, tq)
    h_ = DIFF_HEADS
    h0, nh = heads[0], heads[1] - heads[0]
    return pl.pallas_call(
        functools.partial(_diff_kernel, tq=tq, nq=nq, qps=qps, banded=banded, lambda_init=lambda_init),
        out_shape=jax.ShapeDtypeStruct((b, t, nh * DIFF_DV), F32),
        grid=(b, nh, ns),
        in_specs=[pl.BlockSpec((1, qps * tq, LANES), lambda bi, h, si: (bi, si, h0 + h)),
                  pl.BlockSpec((1, tq, LANES), lambda bi, h, si: (bi, jnp.minimum((si + 1) * qps, nq - 1), h0 + h)),
                  pl.BlockSpec((1, t, LANES), lambda bi, h, si: (bi, 0, h0 + h)),
                  pl.BlockSpec((1, t, LANES), lambda bi, h, si: (bi, 0, h_ + h0 + h)),
                  pl.BlockSpec((1, t, LANES), lambda bi, h, si: (bi, 0, 2 * h_ + h0 + h)),
                  pl.BlockSpec((1, nq, LANES, tq), lambda bi, h, si: (h0 + h, 0, 0, 0)),
                  pl.BlockSpec((1, qps * tq, LANES), lambda bi, h, si: (h0 + h, si, 0)),
                  pl.BlockSpec((1, 2 * tq, tq), lambda bi, h, si: (h0 + h, 0, 0)),
                  pl.BlockSpec((1, 1, LANES), lambda bi, h, si: (h0 + h, 0, 0)),
                  pl.BlockSpec((4, DIFF_HD), lambda bi, h, si: (0, 0)),
                  pl.BlockSpec((1, DIFF_DV), lambda bi, h, si: (0, 0))],
        out_specs=pl.BlockSpec((1, qps * tq, LANES), lambda bi, h, si: (bi, si, h)),
        scratch_shapes=[pltpu.VMEM((nq, 2 * LANES, tq), BF16),
                        pltpu.VMEM((t, 2 * LANES), BF16),
                        pltpu.VMEM((qps, 2, 2 * tq, 2 * LANES), BF16),
                        pltpu.VMEM((qps, 2 * tq, 2 * LANES), F32),
                        pltpu.VMEM((2, 2 * tq, tq), F32),
                        pltpu.SMEM((1,), jnp.int32)],
        compiler_params=_cparams(("parallel", "parallel", "arbitrary")),
        name="diff_attn_banded" if banded else "diff_attn",
    )(qkv, qkv, qkv, qkv, qkv, kaug, qaug, dtab, slope, lam_params.astype(F32),
      g_norm.reshape(1, DIFF_DV).astype(F32))


def _cross_kernel(q_ref, kv_ref, o_ref):
    for h in range(CROSS_HEADS):
        qh = q_ref[0, :, h * CROSS_HD:(h + 1) * CROSS_HD]
        kh = kv_ref[0, :, h * CROSS_HD:(h + 1) * CROSS_HD]
        vh = kv_ref[0, :, CROSS_Q + h * CROSS_HD:CROSS_Q + (h + 1) * CROSS_HD]
        s = lax.dot_general(qh, kh, _NT, preferred_element_type=F32) * (CROSS_HD ** -0.5)
        p = jnp.exp(s - jnp.max(s, axis=-1, keepdims=True))
        l = jnp.sum(p, axis=-1, keepdims=True)
        o = jnp.dot(p.astype(BF16), vh, preferred_element_type=F32)
        o_ref[0, :, h * CROSS_HD:(h + 1) * CROSS_HD] = o / l


def cross_attn(q, kv, tq):
    b, t, _ = q.shape
    return pl.pallas_call(
        _cross_kernel,
        out_shape=jax.ShapeDtypeStruct((b, t, CROSS_Q), F32),
        grid=(b, t // tq),
        in_specs=[pl.BlockSpec((1, tq, CROSS_Q), lambda bi, i: (bi, i, 0)),
                  pl.BlockSpec((1, N_MEM, 2 * CROSS_Q), lambda bi, i: (bi, 0, 0))],
        out_specs=pl.BlockSpec((1, tq, CROSS_Q), lambda bi, i: (bi, i, 0)),
        compiler_params=_cparams(("parallel", "parallel")),
        name="cross_attn",
    )(q, kv)


def _silu(z):
    return z / (1.0 + jnp.exp(-z))


def _sigmoid(z):
    return 1.0 / (1.0 + jnp.exp(-z))


def _merge_kernel(x_ref, of_ref, ob_ref, od_ref, oc_ref, ng_ref, wg_ref, gn_ref, wb_ref, wo_ref, fg_ref, o_ref,
                  *, final):
    x = x_ref[...]
    h = ((x * lax.rsqrt(jnp.mean(x * x, axis=-1, keepdims=True) + EPS)) * ng_ref[...]).astype(BF16)
    og = of_ref[...] + ob_ref[...]
    parts = []
    for hd in range(GLA_HEADS):
        oh = og[:, hd * GLA_DV:(hd + 1) * GLA_DV]
        parts.append(oh * lax.rsqrt(jnp.mean(oh * oh, axis=-1, keepdims=True) + EPS) * gn_ref[...])
    branches = (jnp.concatenate(parts, axis=1), od_ref[...], oc_ref[...])
    merged = None
    for i in range(N_BRANCH):
        z = jnp.dot(h, wg_ref[:, i * BRANCH_W:(i + 1) * BRANCH_W], preferred_element_type=F32)
        a = (branches[i] * _silu(z)).astype(BF16)
        y = jnp.dot(a, wb_ref[i], preferred_element_type=F32)
        c0 = N_BRANCH * BRANCH_W + i * D_MODEL
        gate = _sigmoid(jnp.dot(h, wg_ref[:, c0:c0 + D_MODEL], preferred_element_type=F32))
        merged = gate * y if merged is None else merged + gate * y
    x = x + jnp.dot(merged.astype(BF16), wo_ref[...], preferred_element_type=F32)
    if final:
        x = (x * lax.rsqrt(jnp.mean(x * x, axis=-1, keepdims=True) + EPS)) * fg_ref[...]
    o_ref[...] = x


def merge(x2d, of, ob, od, oc, norm_g, w_gates, gla_norm_g, wb, wo, final_g, final, tm):
    m, d = x2d.shape
    row = lambda w: pl.BlockSpec((tm, w), lambda i: (i, 0))
    return pl.pallas_call(
        functools.partial(_merge_kernel, final=final),
        out_shape=jax.ShapeDtypeStruct((m, d), F32),
        grid=(m // tm,),
        in_specs=[row(d), row(BRANCH_W), row(BRANCH_W), row(BRANCH_W), row(BRANCH_W),
                  _resident((1, d)), _resident(w_gates.shape), _resident((1, GLA_DV)),
                  _resident(wb.shape), _resident(wo.shape), _resident((1, d))],
        out_specs=row(d),
        compiler_params=_cparams(("parallel",)),
        name="merge_final" if final else "merge",
    )(x2d, of, ob, od, oc, norm_g.reshape(1, d).astype(F32), w_gates,
      gla_norm_g.reshape(1, GLA_DV).astype(F32), wb, wo, final_g.reshape(1, d).astype(F32))


def _split_w_in(w_in_l):
    offs = np.concatenate([[0], np.cumsum(np.array(SPLITS))])
    col = lambda i: w_in_l[:, int(offs[i]):int(offs[i + 1])]
    g_q, g_k, g_v, g_lr, z_a, d_q, d_k, d_v, z_b, c_q, z_c, gate_in = (col(i) for i in range(len(SPLITS)))
    lr_pad = jnp.pad(g_lr, ((0, 0), (0, GLR_PAD - 2 * GLA_RANK)))
    w_branch_in = jnp.concatenate([g_q, g_k, g_v, lr_pad, d_q, d_k, d_v, c_q], axis=1).astype(BF16)
    w_gates = jnp.concatenate([z_a, z_b, z_c, gate_in], axis=1).astype(BF16)
    return w_branch_in, w_gates


def _gate_weights(w2_l, b2_l, direction):
    w2pad = jnp.zeros((GLR_PAD, GLA_Q), F32)
    w2pad = w2pad.at[direction * GLA_RANK:(direction + 1) * GLA_RANK].set(w2_l[direction].astype(F32))
    return w2pad.astype(BF16), b2_l[direction].reshape(1, GLA_Q).astype(F32)


def _tiles(t):
    tq = min(t // 2, 512)
    qps = 2 if t // tq <= 4 else 1
    band_heads = DIFF_HEADS // 2 if t // tq > 4 else 0
    return dict(tm=512, tb=min(t, 1024), tq=tq, qps=qps, band_heads=band_heads, tc=min(t, 512))


def _trunk(x, mem, norm_g, w_in, gla_gate_w2, gla_gate_b, gla_norm_g, diff_lambda, diff_norm_g,
           mem_norm_g, w_mem_kv, w_branch, w_out, final_norm_g):
    b, t, d = x.shape
    tl = _tiles(t)
    x2d = x.reshape(b * t, d)
    mem2d = mem.reshape(b * N_MEM, d)
    for l in range(DEPTH):
        lambda_init = 0.8 - 0.6 * math.exp(-0.3 * l)
        w_branch_in, w_gates = _split_w_in(w_in[l])
        g_in, qkv, cq = branch_inputs(x2d, norm_g[l], w_branch_in, tl["tm"])
        g_in, qkv, cq = g_in.reshape(b, t, GLA_IN), qkv.reshape(b, t, 3 * DIFF_QK), cq.reshape(b, t, CROSS_Q)
        kv = norm_matmul(mem2d, mem_norm_g[l], w_mem_kv[l].astype(BF16), BF16, N_MEM, 2 * CROSS_Q)
        kv = kv.reshape(b, N_MEM, 2 * CROSS_Q)
        o_f = gla(g_in, *_gate_weights(gla_gate_w2[l], gla_gate_b[l], 0), False, tl["tb"])
        o_b = gla(g_in, *_gate_weights(gla_gate_w2[l], gla_gate_b[l], 1), True, tl["tb"])
        if tl["band_heads"]:
            hb = tl["band_heads"]
            o_d = jnp.concatenate(
                [diff_attn(qkv, diff_lambda[l], diff_norm_g[l], lambda_init, tl["tq"], 1, (0, hb), True),
                 diff_attn(qkv, diff_lambda[l], diff_norm_g[l], lambda_init, tl["tq"], 1, (hb, DIFF_HEADS))], axis=-1)
        else:
            o_d = diff_attn(qkv, diff_lambda[l], diff_norm_g[l], lambda_init, tl["tq"], tl["qps"])
        o_c = cross_attn(cq, kv, tl["tc"])
        x2d = merge(x2d, o_f.reshape(b * t, GLA_V), o_b.reshape(b * t, GLA_V), o_d.reshape(b * t, DIFF_V),
                    o_c.reshape(b * t, CROSS_Q), norm_g[l], w_gates, gla_norm_g[l], w_branch[l].astype(BF16),
                    w_out[l].astype(BF16), final_norm_g, l == DEPTH - 1, tl["tm"])
    return x2d.reshape(b, t, d)


def kernel(x_prompt, x_sample, mem_prompt, mem_sample, norm_g, w_in, gla_gate_w2, gla_gate_b, gla_norm_g,
           diff_lambda, diff_norm_g, mem_norm_g, w_mem_kv, w_branch, w_out, final_norm_g):
    params = (norm_g, w_in, gla_gate_w2, gla_gate_b, gla_norm_g, diff_lambda, diff_norm_g, mem_norm_g,
              w_mem_kv, w_branch, w_out, final_norm_g)
    return (_trunk(x_prompt, mem_prompt, *params), _trunk(x_sample, mem_sample, *params))
```

```python
import functools
import math

import numpy as np
import jax
import jax.numpy as jnp
from jax import lax
from jax.experimental import pallas as pl
from jax.experimental.pallas import tpu as pltpu

F32 = jnp.float32
BF16 = jnp.bfloat16

D_MODEL = 1024
DEPTH = 4
N_MEM = 256
BRANCH_W = 512
N_BRANCH = 3
EPS = 1e-6
GLA_HEADS = 4
GLA_DK = 64
GLA_DV = 128
GLA_RANK = 16
GLA_TAU = 16.0
GLA_Q = GLA_HEADS * GLA_DK
GLA_V = GLA_HEADS * GLA_DV
DIFF_HEADS = 4
DIFF_HD = 64
DIFF_DV = 2 * DIFF_HD
DIFF_QK = DIFF_HEADS * 2 * DIFF_HD
DIFF_V = DIFF_HEADS * DIFF_DV
CROSS_HEADS = 4
CROSS_HD = 128
CROSS_Q = CROSS_HEADS * CROSS_HD
SPLITS = (GLA_Q, GLA_Q, GLA_V, 2 * GLA_RANK, BRANCH_W,
          DIFF_QK, DIFF_QK, DIFF_V, BRANCH_W,
          CROSS_Q, BRANCH_W, N_BRANCH * D_MODEL)

LANES = 128
LOG2E = 1.4426950408889634
EXP2_ZERO_BELOW = -150.0
SUM_SLACK = 0.01
VMEM_LIMIT = 56 * 1024 * 1024
GLA_CHUNK = 128
GLA_LEVELS = 7
GLR_PAD = LANES
GLA_IN = 2 * GLA_Q + GLA_V + GLR_PAD

_NT = (((1,), (1,)), ((), ()))
_TN = (((0,), (0,)), ((), ()))


def _cparams(sem):
    return pltpu.CompilerParams(dimension_semantics=sem, vmem_limit_bytes=VMEM_LIMIT)


def _resident(shape):
    return pl.BlockSpec(shape, lambda *_: (0,) * len(shape), pipeline_mode=pl.Buffered(1))


def _norm_matmul_kernel(x_ref, g_ref, w_ref, o_ref, h_ref):
    @pl.when(pl.program_id(1) == 0)
    def _():
        x = x_ref[...]
        y = x * lax.rsqrt(jnp.mean(x * x, axis=-1, keepdims=True) + EPS)
        h_ref[...] = (y * g_ref[...]).astype(BF16)

    o_ref[...] = jnp.dot(h_ref[...], w_ref[...], preferred_element_type=F32).astype(o_ref.dtype)


def norm_matmul(x2d, g, w_bf16, out_dtype, tm, tn):
    m, d = x2d.shape
    n = w_bf16.shape[1]
    return pl.pallas_call(
        _norm_matmul_kernel,
        out_shape=jax.ShapeDtypeStruct((m, n), out_dtype),
        grid=(m // tm, n // tn),
        in_specs=[pl.BlockSpec((tm, d), lambda i, j: (i, 0)),
                  pl.BlockSpec((1, d), lambda i, j: (0, 0)),
                  pl.BlockSpec((d, tn), lambda i, j: (0, j))],
        out_specs=pl.BlockSpec((tm, tn), lambda i, j: (i, j)),
        scratch_shapes=[pltpu.VMEM((tm, d), BF16)],
        compiler_params=_cparams(("parallel", "arbitrary")),
        name="norm_matmul",
    )(x2d, g.reshape(1, d).astype(F32), w_bf16)


def _proj_kernel(x_ref, g_ref, w_ref, gla_ref, qkv_ref, cq_ref):
    x = x_ref[...]
    h = ((x * lax.rsqrt(jnp.mean(x * x, axis=-1, keepdims=True) + EPS)) * g_ref[...]).astype(BF16)
    c1, c2 = GLA_IN, GLA_IN + 3 * DIFF_QK
    gla_ref[...] = jnp.dot(h, w_ref[:, 0:c1], preferred_element_type=F32)
    qkv_ref[...] = jnp.dot(h, w_ref[:, c1:c2], preferred_element_type=F32).astype(BF16)
    cq_ref[...] = jnp.dot(h, w_ref[:, c2:c2 + CROSS_Q], preferred_element_type=F32).astype(BF16)


def branch_inputs(x2d, g, w_bf16, tm):
    m, d = x2d.shape
    row = lambda w: pl.BlockSpec((tm, w), lambda i: (i, 0))
    return pl.pallas_call(
        _proj_kernel,
        out_shape=(jax.ShapeDtypeStruct((m, GLA_IN), F32),
                   jax.ShapeDtypeStruct((m, 3 * DIFF_QK), BF16),
                   jax.ShapeDtypeStruct((m, CROSS_Q), BF16)),
        grid=(m // tm,),
        in_specs=[row(d), _resident((1, d)), _resident(w_bf16.shape)],
        out_specs=(row(GLA_IN), row(3 * DIFF_QK), row(CROSS_Q)),
        compiler_params=_cparams(("parallel",)),
        name="branch_inputs",
    )(x2d, g.reshape(1, d).astype(F32), w_bf16)


def _gla_constants(reverse):
    c = GLA_CHUNK
    t = np.arange(c)[:, None]
    u = np.arange(c)[None, :]
    start, end, total = (u <= t), (u > t), np.ones((c, c), bool)
    levels, masks = [], [(t == u)]
    for lvl in range(1, GLA_LEVELS + 1):
        bs, half = 1 << lvl, 1 << (lvl - 1)
        boundary = (t // bs) * bs + half - 1
        right = (t % bs) >= half
        levels.append(np.where(right, (u > boundary) & (u <= t), (u > t) & (u <= boundary)))
        masks.append(((u // bs) == (t // bs)) & right & ((u % bs) < half))
    flip = (lambda m: m[::-1, ::-1]) if reverse else (lambda m: m)
    mall_t = np.concatenate([flip(m).T for m in [start, end] + levels + [total]], axis=1).astype(np.float32)
    mask = np.stack([flip(m) for m in masks], axis=0).astype(np.float32)
    mask = np.concatenate([mask, mask], axis=-1)
    return jnp.asarray(mall_t, BF16), jnp.asarray(np.ascontiguousarray(mask), F32)


def _log_decay(logits):
    return (jnp.minimum(logits, 0.0) - jnp.log1p(jnp.exp(-jnp.abs(logits)))) * (1.0 / GLA_TAU)


def _gla_kernel(g_ref, mt_ref, mask_ref, w2t_ref, b2t_ref, o_ref, st_ref, *, reverse, n_chunks):
    c = GLA_CHUNK

    @pl.when(pl.program_id(1) == 0)
    def _():
        st_ref[...] = jnp.zeros_like(st_ref)

    lane_v = lax.broadcasted_iota(jnp.int32, (c, 2 * GLA_DV), 1)
    vlo, vhi = lane_v < GLA_DV, lane_v >= GLA_DV
    drow = lax.broadcasted_iota(jnp.int32, (LANES, c), 0)
    dlo, dhi = drow < GLA_DK, drow >= GLA_DK
    state_mask = ((lax.broadcasted_iota(jnp.int32, (LANES, 2 * GLA_DV), 0) < GLA_DK)
                  == (lax.broadcasted_iota(jnp.int32, (LANES, 2 * GLA_DV), 1) < GLA_DV))

    def chunk(ci, carry):
        idx = (n_chunks - 1 - ci) if reverse else ci
        rows = pl.ds(pl.multiple_of(idx * c, c), c)
        glr_t = g_ref[0, rows, 2 * GLA_Q + GLA_V:GLA_IN].T.astype(BF16)
        log_at = _log_decay(jnp.dot(w2t_ref[...], glr_t, preferred_element_type=F32) + b2t_ref[...])
        e_all = jnp.exp(jnp.dot(log_at.astype(BF16), mt_ref[...], preferred_element_type=F32))
        for p in range(GLA_HEADS // 2):
            q_t = g_ref[0, rows, p * LANES:(p + 1) * LANES].T * (GLA_DK ** -0.5)
            k_t = g_ref[0, rows, GLA_Q + p * LANES:GLA_Q + (p + 1) * LANES].T
            v = g_ref[0, rows, 2 * GLA_Q + p * 2 * GLA_DV:2 * GLA_Q + (p + 1) * 2 * GLA_DV]
            v2 = jnp.concatenate([jnp.where(vlo, v, 0.0), jnp.where(vhi, v, 0.0)], axis=0).astype(BF16)
            e = lambda blk: e_all[p * LANES:(p + 1) * LANES, blk * c:(blk + 1) * c]
            st = st_ref[p]
            o = lax.dot_general((q_t * e(0)).astype(BF16), st.astype(BF16), _TN, preferred_element_type=F32)
            att = jnp.zeros((c, 2 * c), F32)
            for lvl in range(GLA_LEVELS + 1):
                qs, ks = (q_t, k_t) if lvl == 0 else (q_t * e(lvl + 1), k_t * e(lvl + 1))
                k2 = jnp.concatenate([jnp.where(dlo, ks, 0.0), jnp.where(dhi, ks, 0.0)], axis=1)
                a = lax.dot_general(qs.astype(BF16), k2.astype(BF16), _TN, preferred_element_type=F32)
                att = att + a * mask_ref[lvl]
            o = o + jnp.dot(att.astype(BF16), v2, preferred_element_type=F32)
            o_ref[0, rows, p * 2 * GLA_DV:(p + 1) * 2 * GLA_DV] = o
            upd = jnp.dot((k_t * e(1)).astype(BF16), v.astype(BF16), preferred_element_type=F32)
            decay = e(GLA_LEVELS + 2)
            st_ref[p] = st * jnp.concatenate([decay, decay], axis=1) + jnp.where(state_mask, upd, 0.0)
        return carry

    lax.fori_loop(0, n_chunks, chunk, 0, unroll=True)


def gla(g_in, w2pad, b2, reverse, tb):
    b, t, _ = g_in.shape
    nblk = t // tb
    mall_t, mask = _gla_constants(reverse)
    b2t = jnp.broadcast_to(b2.reshape(GLA_Q, 1), (GLA_Q, GLA_CHUNK))
    tmap = (lambda bi, i: (bi, nblk - 1 - i, 0)) if reverse else (lambda bi, i: (bi, i, 0))
    consts = (mall_t, mask, w2pad.T, b2t)
    return pl.pallas_call(
        functools.partial(_gla_kernel, reverse=reverse, n_chunks=tb // GLA_CHUNK),
        out_shape=jax.ShapeDtypeStruct((b, t, GLA_V), F32),
        grid=(b, nblk),
        in_specs=[pl.BlockSpec((1, tb, GLA_IN), tmap)] + [_resident(a.shape) for a in consts],
        out_specs=pl.BlockSpec((1, tb, GLA_V), tmap),
        scratch_shapes=[pltpu.VMEM((GLA_HEADS // 2, LANES, 2 * GLA_DV), F32)],
        compiler_params=_cparams(("parallel", "arbitrary")),
        name="gla_bwd" if reverse else "gla_fwd",
    )(g_in, *consts)


def _bf16_terms(x, n=3):
    terms, rest = [], np.asarray(x, np.float64)
    for _ in range(n):
        bits = rest.astype(np.float32).view(np.uint32)
        bits = (bits + np.uint32(0x7FFF) + ((bits >> np.uint32(16)) & np.uint32(1))) & np.uint32(0xFFFF0000)
        term = bits.view(np.float32)
        terms.append(term)
        rest = rest - term.astype(np.float64)
    return terms


def _alibi_constants(t, tq):
    slopes = LOG2E * 2.0 ** (-8.0 * (np.arange(DIFF_HEADS, dtype=np.float64) + 1.0) / DIFF_HEADS)
    pos = np.arange(t)
    parts = ((pos // LANES) * LANES, pos % LANES)
    n = 3 * len(parts)
    kaug = np.zeros((DIFF_HEADS, t, LANES), np.float32)
    qaug = np.zeros((DIFF_HEADS, t, LANES), np.float32)
    kaug[:, :, n:2 * n] = 1.0
    qaug[:, :, 0:n] = 1.0
    for h in range(DIFF_HEADS):
        for j, part in enumerate(parts):
            for i, term in enumerate(_bf16_terms(slopes[h] * part)):
                kaug[h, :, 3 * j + i] = -term
                qaug[h, :, n + 3 * j + i] = term
    i = np.arange(tq)
    dtab = -slopes[:, None, None] * np.abs(i[:, None] - i[None, :])[None]
    dtab = np.concatenate([dtab, dtab], axis=1)
    kaug_t = kaug.reshape(DIFF_HEADS, t // tq, tq, LANES).transpose(0, 1, 3, 2)
    slope_rows = np.broadcast_to(slopes[:, None, None], (DIFF_HEADS, 1, LANES))
    return jnp.asarray(kaug_t, BF16), jnp.asarray(qaug, BF16), jnp.asarray(dtab, F32), jnp.asarray(slope_rows, F32)


def _diff_kernel(q_ref, qn_ref, qf_ref, k_ref, v_ref, kaug_ref, qaug_ref, dtab_ref, slope_ref, lp_ref, gn_ref,
                 o_ref, kp_ref, vp_ref, lhs_ref, acc_ref, s_ref, reach_ref, *, tq, nq, qps, banded, lambda_init):
    step = pl.program_id(2)
    nk = nq
    assert nk % 2 == 0
    lane = lax.broadcasted_iota(jnp.int32, (tq, LANES), 1)

    def stacked_q(q):
        qs = q * (LOG2E * DIFF_HD ** -0.5)
        return jnp.concatenate([jnp.where(lane < DIFF_HD, qs, 0.0), jnp.where(lane >= DIFF_HD, qs, 0.0)],
                               axis=0).astype(BF16)

    def key_rows(kj):
        return pl.ds(pl.multiple_of(kj * tq, tq), tq)

    def diagonal_scores(q, kj):
        lhs = jnp.concatenate([stacked_q(q), jnp.zeros((2 * tq, LANES), BF16)], axis=1)
        return jnp.dot(lhs, kp_ref[kj], preferred_element_type=F32) + dtab_ref[0]

    def half_sums(x):
        half = (lax.broadcasted_iota(jnp.int32, (LANES, LANES), 0) < DIFF_HD) == (
            lax.broadcasted_iota(jnp.int32, (LANES, LANES), 1) < DIFF_HD)
        return jnp.dot(x.astype(BF16), jnp.where(half, 1.0, 0.0).astype(BF16), preferred_element_type=F32)

    def overall(reduce, x):
        return reduce(reduce(x, axis=0, keepdims=True), axis=1, keepdims=True)

    def reach_blocks():
        qs = (qf_ref[0] * (LOG2E * DIFF_HD ** -0.5)).astype(F32)
        ks = k_ref[0].astype(F32)
        bound = (1.0 + 2.0 * SUM_SLACK) * jnp.sqrt(
            overall(jnp.max, half_sums(qs * qs)) * overall(jnp.max, half_sums(ks * ks)))
        m_low = overall(jnp.min, half_sums(qs * ks) - SUM_SLACK * half_sums(jnp.abs(qs * ks)) - 0.1)
        blocks = ((bound - m_low - EXP2_ZERO_BELOW) / slope_ref[0, :, 0:1] - 1.0) * (1.0 / tq)
        return jnp.where(blocks < nk, jnp.floor(blocks) + 1.0, float(nk)).astype(jnp.int32)[0, 0]

    def new_head():
        for kb in range(nk):
            kp_ref[kb, 0:LANES, :] = k_ref[0, kb * tq:(kb + 1) * tq, :].T
            kp_ref[kb, LANES:2 * LANES, :] = kaug_ref[0, kb]
        vp_ref[:, 0:LANES] = v_ref[0]
        vp_ref[:, LANES:2 * LANES] = jnp.ones((vp_ref.shape[0], LANES), BF16)
        s_ref[0] = diagonal_scores(q_ref[0, 0:tq, :], 0)
        if banded:
            reach_ref[0] = reach_blocks()

    if nq == qps:
        new_head()
    else:
        pl.when(step == 0)(new_head)

    lp = lp_ref[...]
    lam = (jnp.exp(jnp.sum(lp[0:1] * lp[1:2], axis=-1, keepdims=True))
           - jnp.exp(jnp.sum(lp[2:3] * lp[3:4], axis=-1, keepdims=True)) + lambda_init)

    for j in range(qps):
        qi = step * qps + j
        rows_j = slice(j * tq, (j + 1) * tq)
        lhs, acc = lhs_ref.at[j], acc_ref.at[j]
        qa = jnp.concatenate([qaug_ref[0, rows_j, :], qaug_ref[0, rows_j, :]], axis=0)
        lhs[:, :, 0:LANES] = jnp.broadcast_to(stacked_q(q_ref[0, rows_j, :])[None], (2, 2 * tq, LANES))
        lhs[0, :, LANES:2 * LANES] = -qa
        lhs[1, :, LANES:2 * LANES] = qa

        if banded:
            reach = reach_ref[0]
            lo, hi = jnp.maximum(qi - reach, 0), jnp.minimum(qi + reach, nk - 1)
            odd = (hi - lo) % 2 == 0
            grow_hi = odd & (hi < nk - 1)
            lo, hi = jnp.where(odd & ~grow_hi, lo - 1, lo), jnp.where(grow_hi, hi + 1, hi)
            count = hi - lo + 1
        else:
            lo, count = 0, nk

        def accumulate(s, kj, m, acc=acc):
            m_new = jnp.max(s, axis=-1, keepdims=True)
            if m is not None:
                m_new = jnp.maximum(m, m_new)
            pv = jnp.dot(jnp.exp2(s - m_new).astype(BF16), vp_ref[key_rows(kj), :], preferred_element_type=F32)
            acc[...] = pv if m is None else acc[...] * jnp.exp2(m - m_new) + pv
            return m_new

        def block(n, qi=qi, lo=lo):
            return jnp.where(n == 0, qi, jnp.where(lo + n - 1 < qi, lo + n - 1, lo + n))

        def other_scores(n, qi=qi, lhs=lhs, block=block):
            kj = block(n)
            return jnp.dot(lhs[jnp.where(kj < qi, 0, 1)], kp_ref[kj], preferred_element_type=F32)

        def pair(t, m, accumulate=accumulate, block=block, other_scores=other_scores):
            s_ref[0] = other_scores(2 * t)
            m = accumulate(s_ref[1], block(2 * t - 1), m)
            s_ref[1] = other_scores(2 * t + 1)
            return accumulate(s_ref[0], block(2 * t), m)

        s_ref[1] = other_scores(1)
        m = accumulate(s_ref[0], qi, None)
        if banded:
            m = lax.fori_loop(1, count // 2, pair, m)
        else:
            m = lax.fori_loop(1, count // 2, pair, m, unroll=True)
        if j + 1 < qps:
            s_ref[0] = diagonal_scores(q_ref[0, (j + 1) * tq:(j + 2) * tq, :], qi + 1)
        else:
            s_ref[0] = diagonal_scores(qn_ref[0], jnp.minimum(qi + 1, nq - 1))
        accumulate(s_ref[1], block(count - 1), m)

        a1, a2 = acc[0:tq], acc[tq:2 * tq]
        o = a1[:, 0:LANES] / a1[:, LANES:2 * LANES] - lam * (a2[:, 0:LANES] / a2[:, LANES:2 * LANES])
        y = o * lax.rsqrt(jnp.mean(o * o, axis=-1, keepdims=True) + EPS)
        o_ref[0, rows_j, :] = (y * gn_ref[...]) * (1.0 - lambda_init)


def diff_attn(qkv, lam_params, g_norm, lambda_init, tq, qps, heads=(0, DIFF_HEADS), banded=False):
    b, t, _ = qkv.shape
    nq = t // tq
    ns = nq // qps
    kaug, qaug, dtab, slope = _alibi_constants(t, tq)
    h_ = DIFF_HEADS
    h0, nh = heads[0], heads[1] - heads[0]
    return pl.pallas_call(
        functools.partial(_diff_kernel, tq=tq, nq=nq, qps=qps, banded=banded, lambda_init=lambda_init),
        out_shape=jax.ShapeDtypeStruct((b, t, nh * DIFF_DV), F32),
        grid=(b, nh, ns),
        in_specs=[pl.BlockSpec((1, qps * tq, LANES), lambda bi, h, si: (bi, si, h0 + h)),
                  pl.BlockSpec((1, tq, LANES), lambda bi, h, si: (bi, jnp.minimum((si + 1) * qps, nq - 1), h0 + h)),
                  pl.BlockSpec((1, t, LANES), lambda bi, h, si: (bi, 0, h0 + h)),
                  pl.BlockSpec((1, t, LANES), lambda bi, h, si: (bi, 0, h_ + h0 + h)),
                  pl.BlockSpec((1, t, LANES), lambda bi, h, si: (bi, 0, 2 * h_ + h0 + h)),
                  pl.BlockSpec((1, nq, LANES, tq), lambda bi, h, si: (h0 + h, 0, 0, 0)),
                  pl.BlockSpec((1, qps * tq, LANES), lambda bi, h, si: (h0 + h, si, 0)),
                  pl.BlockSpec((1, 2 * tq, tq), lambda bi, h, si: (h0 + h, 0, 0)),
                  pl.BlockSpec((1, 1, LANES), lambda bi, h, si: (h0 + h, 0, 0)),
                  pl.BlockSpec((4, DIFF_HD), lambda bi, h, si: (0, 0)),
                  pl.BlockSpec((1, DIFF_DV), lambda bi, h, si: (0, 0))],
        out_specs=pl.BlockSpec((1, qps * tq, LANES), lambda bi, h, si: (bi, si, h)),
        scratch_shapes=[pltpu.VMEM((nq, 2 * LANES, tq), BF16),
                        pltpu.VMEM((t, 2 * LANES), BF16),
                        pltpu.VMEM((qps, 2, 2 * tq, 2 * LANES), BF16),
                        pltpu.VMEM((qps, 2 * tq, 2 * LANES), F32),
                        pltpu.VMEM((2, 2 * tq, tq), F32),
                        pltpu.SMEM((1,), jnp.int32)],
        compiler_params=_cparams(("parallel", "parallel", "arbitrary")),
        name="diff_attn_banded" if banded else "diff_attn",
    )(qkv, qkv, qkv, qkv, qkv, kaug, qaug, dtab, slope, lam_params.astype(F32),
      g_norm.reshape(1, DIFF_DV).astype(F32))


def _cross_kernel(q_ref, kv_ref, o_ref):
    for h in range(CROSS_HEADS):
        qh = q_ref[0, :, h * CROSS_HD:(h + 1) * CROSS_HD]
        kh = kv_ref[0, :, h * CROSS_HD:(h + 1) * CROSS_HD]
        vh = kv_ref[0, :, CROSS_Q + h * CROSS_HD:CROSS_Q + (h + 1) * CROSS_HD]
        s = lax.dot_general(qh, kh, _NT, preferred_element_type=F32) * (CROSS_HD ** -0.5)
        p = jnp.exp(s - jnp.max(s, axis=-1, keepdims=True))
        l = jnp.sum(p, axis=-1, keepdims=True)
        o = jnp.dot(p.astype(BF16), vh, preferred_element_type=F32)
        o_ref[0, :, h * CROSS_HD:(h + 1) * CROSS_HD] = o / l


def cross_attn(q, kv, tq):
    b, t, _ = q.shape
    return pl.pallas_call(
        _cross_kernel,
        out_shape=jax.ShapeDtypeStruct((b, t, CROSS_Q), F32),
        grid=(b, t // tq),
        in_specs=[pl.BlockSpec((1, tq, CROSS_Q), lambda bi, i: (bi, i, 0)),
                  pl.BlockSpec((1, N_MEM, 2 * CROSS_Q), lambda bi, i: (bi, 0, 0))],
        out_specs=pl.BlockSpec((1, tq, CROSS_Q), lambda bi, i: (bi, i, 0)),
        compiler_params=_cparams(("parallel", "parallel")),
        name="cross_attn",
    )(q, kv)


def _silu(z):
    return z / (1.0 + jnp.exp(-z))


def _sigmoid(z):
    return 1.0 / (1.0 + jnp.exp(-z))


def _merge_kernel(x_ref, of_ref, ob_ref, od_ref, oc_ref, ng_ref, wg_ref, gn_ref, wb_ref, wo_ref, fg_ref, o_ref,
                  *, final):
    x = x_ref[...]
    h = ((x * lax.rsqrt(jnp.mean(x * x, axis=-1, keepdims=True) + EPS)) * ng_ref[...]).astype(BF16)
    og = of_ref[...] + ob_ref[...]
    parts = []
    for hd in range(GLA_HEADS):
        oh = og[:, hd * GLA_DV:(hd + 1) * GLA_DV]
        parts.append(oh * lax.rsqrt(jnp.mean(oh * oh, axis=-1, keepdims=True) + EPS) * gn_ref[...])
    branches = (jnp.concatenate(parts, axis=1), od_ref[...], oc_ref[...])
    merged = None
    for i in range(N_BRANCH):
        z = jnp.dot(h, wg_ref[:, i * BRANCH_W:(i + 1) * BRANCH_W], preferred_element_type=F32)
        a = (branches[i] * _silu(z)).astype(BF16)
        y = jnp.dot(a, wb_ref[i], preferred_element_type=F32)
        c0 = N_BRANCH * BRANCH_W + i * D_MODEL
        gate = _sigmoid(jnp.dot(h, wg_ref[:, c0:c0 + D_MODEL], preferred_element_type=F32))
        merged = gate * y if merged is None else merged + gate * y
    x = x + jnp.dot(merged.astype(BF16), wo_ref[...], preferred_element_type=F32)
    if final:
        x = (x * lax.rsqrt(jnp.mean(x * x, axis=-1, keepdims=True) + EPS)) * fg_ref[...]
    o_ref[...] = x


def merge(x2d, of, ob, od, oc, norm_g, w_gates, gla_norm_g, wb, wo, final_g, final, tm):
    m, d = x2d.shape
    row = lambda w: pl.BlockSpec((tm, w), lambda i: (i, 0))
    return pl.pallas_call(
        functools.partial(_merge_kernel, final=final),
        out_shape=jax.ShapeDtypeStruct((m, d), F32),
        grid=(m // tm,),
        in_specs=[row(d), row(BRANCH_W), row(BRANCH_W), row(BRANCH_W), row(BRANCH_W),
                  _resident((1, d)), _resident(w_gates.shape), _resident((1, GLA_DV)),
                  _resident(wb.shape), _resident(wo.shape), _resident((1, d))],
        out_specs=row(d),
        compiler_params=_cparams(("parallel",)),
        name="merge_final" if final else "merge",
    )(x2d, of, ob, od, oc, norm_g.reshape(1, d).astype(F32), w_gates,
      gla_norm_g.reshape(1, GLA_DV).astype(F32), wb, wo, final_g.reshape(1, d).astype(F32))


def _split_w_in(w_in_l):
    offs = np.concatenate([[0], np.cumsum(np.array(SPLITS))])
    col = lambda i: w_in_l[:, int(offs[i]):int(offs[i + 1])]
    g_q, g_k, g_v, g_lr, z_a, d_q, d_k, d_v, z_b, c_q, z_c, gate_in = (col(i) for i in range(len(SPLITS)))
    lr_pad = jnp.pad(g_lr, ((0, 0), (0, GLR_PAD - 2 * GLA_RANK)))
    w_branch_in = jnp.concatenate([g_q, g_k, g_v, lr_pad, d_q, d_k, d_v, c_q], axis=1).astype(BF16)
    w_gates = jnp.concatenate([z_a, z_b, z_c, gate_in], axis=1).astype(BF16)
    return w_branch_in, w_gates


def _gate_weights(w2_l, b2_l, direction):
    w2pad = jnp.zeros((GLR_PAD, GLA_Q), F32)
    w2pad = w2pad.at[direction * GLA_RANK:(direction + 1) * GLA_RANK].set(w2_l[direction].astype(F32))
    return w2pad.astype(BF16), b2_l[direction].reshape(1, GLA_Q).astype(F32)


def _tiles(t):
    tq = min(t // 2, 512)
    qps = 2 if t // tq <= 4 else 1
    band_heads = DIFF_HEADS // 2 if t // tq > 4 else 0
    return dict(tm=512, tb=min(t, 1024), tq=tq, qps=qps, band_heads=band_heads, tc=min(t, 512))


def _trunk(x, mem, norm_g, w_in, gla_gate_w2, gla_gate_b, gla_norm_g, diff_lambda, diff_norm_g,
           mem_norm_g, w_mem_kv, w_branch, w_out, final_norm_g):
    b, t, d = x.shape
    tl = _tiles(t)
    x2d = x.reshape(b * t, d)
    mem2d = mem.reshape(b * N_MEM, d)
    for l in range(DEPTH):
        lambda_init = 0.8 - 0.6 * math.exp(-0.3 * l)
        w_branch_in, w_gates = _split_w_in(w_in[l])
        g_in, qkv, cq = branch_inputs(x2d, norm_g[l], w_branch_in, tl["tm"])
        g_in, qkv, cq = g_in.reshape(b, t, GLA_IN), qkv.reshape(b, t, 3 * DIFF_QK), cq.reshape(b, t, CROSS_Q)
        kv = norm_matmul(mem2d, mem_norm_g[l], w_mem_kv[l].astype(BF16), BF16, N_MEM, 2 * CROSS_Q)
        kv = kv.reshape(b, N_MEM, 2 * CROSS_Q)
        o_f = gla(g_in, *_gate_weights(gla_gate_w2[l], gla_gate_b[l], 0), False, tl["tb"])
        o_b = gla(g_in, *_gate_weights(gla_gate_w2[l], gla_gate_b[l], 1), True, tl["tb"])
        if tl["band_heads"]:
            hb = tl["band_heads"]
            o_d = jnp.concatenate(
                [diff_attn(qkv, diff_lambda[l], diff_norm_g[l], lambda_init, tl["tq"], 1, (0, hb), True),
                 diff_attn(qkv, diff_lambda[l], diff_norm_g[l], lambda_init, tl["tq"], 1, (hb, DIFF_HEADS))], axis=-1)
        else:
            o_d = diff_attn(qkv, diff_lambda[l], diff_norm_g[l], lambda_init, tl["tq"], tl["qps"])
        o_c = cross_attn(cq, kv, tl["tc"])
        x2d = merge(x2d, o_f.reshape(b * t, GLA_V), o_b.reshape(b * t, GLA_V), o_d.reshape(b * t, DIFF_V),
                    o_c.reshape(b * t, CROSS_Q), norm_g[l], w_gates, gla_norm_g[l], w_branch[l].astype(BF16),
                    w_out[l].astype(BF16), final_norm_g, l == DEPTH - 1, tl["tm"])
    return x2d.reshape(b, t, d)


def kernel(x_prompt, x_sample, mem_prompt, mem_sample, norm_g, w_in, gla_gate_w2, gla_gate_b, gla_norm_g,
           diff_lambda, diff_norm_g, mem_norm_g, w_mem_kv, w_branch, w_out, final_norm_g):
    params = (norm_g, w_in, gla_gate_w2, gla_gate_b, gla_norm_g, diff_lambda, diff_norm_g, mem_norm_g,
              w_mem_kv, w_branch, w_out, final_norm_g)
    return (_trunk(x_prompt, mem_prompt, *params), _trunk(x_sample, mem_sample, *params))
```

```python
import functools
import math

import numpy as np
import jax
import jax.numpy as jnp
from jax import lax
from jax.experimental import pallas as pl
from jax.experimental.pallas import tpu as pltpu

F32 = jnp.float32
BF16 = jnp.bfloat16

D_MODEL = 1024
DEPTH = 4
N_MEM = 256
BRANCH_W = 512
N_BRANCH = 3
EPS = 1e-6
GLA_HEADS = 4
GLA_DK = 64
GLA_DV = 128
GLA_RANK = 16
GLA_TAU = 16.0
GLA_Q = GLA_HEADS * GLA_DK
GLA_V = GLA_HEADS * GLA_DV
DIFF_HEADS = 4
DIFF_HD = 64
DIFF_DV = 2 * DIFF_HD
DIFF_QK = DIFF_HEADS * 2 * DIFF_HD
DIFF_V = DIFF_HEADS * DIFF_DV
CROSS_HEADS = 4
CROSS_HD = 128
CROSS_Q = CROSS_HEADS * CROSS_HD
SPLITS = (GLA_Q, GLA_Q, GLA_V, 2 * GLA_RANK, BRANCH_W,
          DIFF_QK, DIFF_QK, DIFF_V, BRANCH_W,
          CROSS_Q, BRANCH_W, N_BRANCH * D_MODEL)

LANES = 128
LOG2E = 1.4426950408889634
EXP2_ZERO_BELOW = -150.0
SUM_SLACK = 0.01
VMEM_LIMIT = 56 * 1024 * 1024
GLA_CHUNK = 128
GLA_LEVELS = 7
GLR_PAD = LANES
GLA_IN = 2 * GLA_Q + GLA_V + GLR_PAD

_NT = (((1,), (1,)), ((), ()))
_TN = (((0,), (0,)), ((), ()))


def _cparams(sem):
    return pltpu.CompilerParams(dimension_semantics=sem, vmem_limit_bytes=VMEM_LIMIT)


def _resident(shape):
    return pl.BlockSpec(shape, lambda *_: (0,) * len(shape), pipeline_mode=pl.Buffered(1))


def _norm_matmul_kernel(x_ref, g_ref, w_ref, o_ref, h_ref):
    @pl.when(pl.program_id(1) == 0)
    def _():
        x = x_ref[...]
        y = x * lax.rsqrt(jnp.mean(x * x, axis=-1, keepdims=True) + EPS)
        h_ref[...] = (y * g_ref[...]).astype(BF16)

    o_ref[...] = jnp.dot(h_ref[...], w_ref[...], preferred_element_type=F32).astype(o_ref.dtype)


def norm_matmul(x2d, g, w_bf16, out_dtype, tm, tn):
    m, d = x2d.shape
    n = w_bf16.shape[1]
    return pl.pallas_call(
        _norm_matmul_kernel,
        out_shape=jax.ShapeDtypeStruct((m, n), out_dtype),
        grid=(m // tm, n // tn),
        in_specs=[pl.BlockSpec((tm, d), lambda i, j: (i, 0)),
                  pl.BlockSpec((1, d), lambda i, j: (0, 0)),
                  pl.BlockSpec((d, tn), lambda i, j: (0, j))],
        out_specs=pl.BlockSpec((tm, tn), lambda i, j: (i, j)),
        scratch_shapes=[pltpu.VMEM((tm, d), BF16)],
        compiler_params=_cparams(("parallel", "arbitrary")),
        name="norm_matmul",
    )(x2d, g.reshape(1, d).astype(F32), w_bf16)


def _proj_kernel(x_ref, g_ref, w_ref, gla_ref, qkv_ref, cq_ref):
    x = x_ref[...]
    h = ((x * lax.rsqrt(jnp.mean(x * x, axis=-1, keepdims=True) + EPS)) * g_ref[...]).astype(BF16)
    c1, c2 = GLA_IN, GLA_IN + 3 * DIFF_QK
    gla_ref[...] = jnp.dot(h, w_ref[:, 0:c1], preferred_element_type=F32)
    qkv_ref[...] = jnp.dot(h, w_ref[:, c1:c2], preferred_element_type=F32).astype(BF16)
    cq_ref[...] = jnp.dot(h, w_ref[:, c2:c2 + CROSS_Q], preferred_element_type=F32).astype(BF16)


def branch_inputs(x2d, g, w_bf16, tm):
    m, d = x2d.shape
    row = lambda w: pl.BlockSpec((tm, w), lambda i: (i, 0))
    return pl.pallas_call(
        _proj_kernel,
        out_shape=(jax.ShapeDtypeStruct((m, GLA_IN), F32),
                   jax.ShapeDtypeStruct((m, 3 * DIFF_QK), BF16),
                   jax.ShapeDtypeStruct((m, CROSS_Q), BF16)),
        grid=(m // tm,),
        in_specs=[row(d), _resident((1, d)), _resident(w_bf16.shape)],
        out_specs=(row(GLA_IN), row(3 * DIFF_QK), row(CROSS_Q)),
        compiler_params=_cparams(("parallel",)),
        name="branch_inputs",
    )(x2d, g.reshape(1, d).astype(F32), w_bf16)


def _gla_constants(reverse):
    c = GLA_CHUNK
    t = np.arange(c)[:, None]
    u = np.arange(c)[None, :]
    start, end, total = (u <= t), (u > t), np.ones((c, c), bool)
    levels, masks = [], [(t == u)]
    for lvl in range(1, GLA_LEVELS + 1):
        bs, half = 1 << lvl, 1 << (lvl - 1)
        boundary = (t // bs) * bs + half - 1
        right = (t % bs) >= half
        levels.append(np.where(right, (u > boundary) & (u <= t), (u > t) & (u <= boundary)))
        masks.append(((u // bs) == (t // bs)) & right & ((u % bs) < half))
    flip = (lambda m: m[::-1, ::-1]) if reverse else (lambda m: m)
    mall_t = np.concatenate([flip(m).T for m in [start, end] + levels + [total]], axis=1).astype(np.float32)
    mask = np.stack([flip(m) for m in masks], axis=0).astype(np.float32)
    mask = np.concatenate([mask, mask], axis=-1)
    return jnp.asarray(mall_t, BF16), jnp.asarray(np.ascontiguousarray(mask), F32)


def _log_decay(logits):
    return (jnp.minimum(logits, 0.0) - jnp.log1p(jnp.exp(-jnp.abs(logits)))) * (1.0 / GLA_TAU)


def _gla_kernel(g_ref, mt_ref, mask_ref, w2t_ref, b2t_ref, o_ref, st_ref, *, reverse, n_chunks):
    c = GLA_CHUNK

    @pl.when(pl.program_id(1) == 0)
    def _():
        st_ref[...] = jnp.zeros_like(st_ref)

    lane_v = lax.broadcasted_iota(jnp.int32, (c, 2 * GLA_DV), 1)
    vlo, vhi = lane_v < GLA_DV, lane_v >= GLA_DV
    drow = lax.broadcasted_iota(jnp.int32, (LANES, c), 0)
    dlo, dhi = drow < GLA_DK, drow >= GLA_DK
    state_mask = ((lax.broadcasted_iota(jnp.int32, (LANES, 2 * GLA_DV), 0) < GLA_DK)
                  == (lax.broadcasted_iota(jnp.int32, (LANES, 2 * GLA_DV), 1) < GLA_DV))

    def chunk(ci, carry):
        idx = (n_chunks - 1 - ci) if reverse else ci
        rows = pl.ds(pl.multiple_of(idx * c, c), c)
        glr_t = g_ref[0, rows, 2 * GLA_Q + GLA_V:GLA_IN].T.astype(BF16)
        log_at = _log_decay(jnp.dot(w2t_ref[...], glr_t, preferred_element_type=F32) + b2t_ref[...])
        e_all = jnp.exp(jnp.dot(log_at.astype(BF16), mt_ref[...], preferred_element_type=F32))
        for p in range(GLA_HEADS // 2):
            q_t = g_ref[0, rows, p * LANES:(p + 1) * LANES].T * (GLA_DK ** -0.5)
            k_t = g_ref[0, rows, GLA_Q + p * LANES:GLA_Q + (p + 1) * LANES].T
            v = g_ref[0, rows, 2 * GLA_Q + p * 2 * GLA_DV:2 * GLA_Q + (p + 1) * 2 * GLA_DV]
            v2 = jnp.concatenate([jnp.where(vlo, v, 0.0), jnp.where(vhi, v, 0.0)], axis=0).astype(BF16)
            e = lambda blk: e_all[p * LANES:(p + 1) * LANES, blk * c:(blk + 1) * c]
            st = st_ref[p]
            o = lax.dot_general((q_t * e(0)).astype(BF16), st.astype(BF16), _TN, preferred_element_type=F32)
            att = jnp.zeros((c, 2 * c), F32)
            for lvl in range(GLA_LEVELS + 1):
                qs, ks = (q_t, k_t) if lvl == 0 else (q_t * e(lvl + 1), k_t * e(lvl + 1))
                k2 = jnp.concatenate([jnp.where(dlo, ks, 0.0), jnp.where(dhi, ks, 0.0)], axis=1)
                a = lax.dot_general(qs.astype(BF16), k2.astype(BF16), _TN, preferred_element_type=F32)
                att = att + a * mask_ref[lvl]
            o = o + jnp.dot(att.astype(BF16), v2, preferred_element_type=F32)
            o_ref[0, rows, p * 2 * GLA_DV:(p + 1) * 2 * GLA_DV] = o
            upd = jnp.dot((k_t * e(1)).astype(BF16), v.astype(BF16), preferred_element_type=F32)
            decay = e(GLA_LEVELS + 2)
            st_ref[p] = st * jnp.concatenate([decay, decay], axis=1) + jnp.where(state_mask, upd, 0.0)
        return carry

    lax.fori_loop(0, n_chunks, chunk, 0, unroll=True)


def gla(g_in, w2pad, b2, reverse, tb):
    b, t, _ = g_in.shape
    nblk = t // tb
    mall_t, mask = _gla_constants(reverse)
    b2t = jnp.broadcast_to(b2.reshape(GLA_Q, 1), (GLA_Q, GLA_CHUNK))
    tmap = (lambda bi, i: (bi, nblk - 1 - i, 0)) if reverse else (lambda bi, i: (bi, i, 0))
    consts = (mall_t, mask, w2pad.T, b2t)
    return pl.pallas_call(
        functools.partial(_gla_kernel, reverse=reverse, n_chunks=tb // GLA_CHUNK),
        out_shape=jax.ShapeDtypeStruct((b, t, GLA_V), F32),
        grid=(b, nblk),
        in_specs=[pl.BlockSpec((1, tb, GLA_IN), tmap)] + [_resident(a.shape) for a in consts],
        out_specs=pl.BlockSpec((1, tb, GLA_V), tmap),
        scratch_shapes=[pltpu.VMEM((GLA_HEADS // 2, LANES, 2 * GLA_DV), F32)],
        compiler_params=_cparams(("parallel", "arbitrary")),
        name="gla_bwd" if reverse else "gla_fwd",
    )(g_in, *consts)


def _bf16_terms(x, n=3):
    terms, rest = [], np.asarray(x, np.float64)
    for _ in range(n):
        bits = rest.astype(np.float32).view(np.uint32)
        bits = (bits + np.uint32(0x7FFF) + ((bits >> np.uint32(16)) & np.uint32(1))) & np.uint32(0xFFFF0000)
        term = bits.view(np.float32)
        terms.append(term)
        rest = rest - term.astype(np.float64)
    return terms


def _alibi_constants(t, tq):
    slopes = LOG2E * 2.0 ** (-8.0 * (np.arange(DIFF_HEADS, dtype=np.float64) + 1.0) / DIFF_HEADS)
    pos = np.arange(t)
    parts = ((pos // LANES) * LANES, pos % LANES)
    n = 3 * len(parts)
    kaug = np.zeros((DIFF_HEADS, t, LANES), np.float32)
    qaug = np.zeros((DIFF_HEADS, t, LANES), np.float32)
    kaug[:, :, n:2 * n] = 1.0
    qaug[:, :, 0:n] = 1.0
    for h in range(DIFF_HEADS):
        for j, part in enumerate(parts):
            for i, term in enumerate(_bf16_terms(slopes[h] * part)):
                kaug[h, :, 3 * j + i] = -term
                qaug[h, :, n + 3 * j + i] = term
    i = np.arange(tq)
    dtab = -slopes[:, None, None] * np.abs(i[:, None] - i[None, :])[None]
    dtab = np.concatenate([dtab, dtab], axis=1)
    kaug_t = kaug.reshape(DIFF_HEADS, t // tq, tq, LANES).transpose(0, 1, 3, 2)
    slope_rows = np.broadcast_to(slopes[:, None, None], (DIFF_HEADS, 1, LANES))
    return jnp.asarray(kaug_t, BF16), jnp.asarray(qaug, BF16), jnp.asarray(dtab, F32), jnp.asarray(slope_rows, F32)


def _diff_kernel(q_ref, qn_ref, qf_ref, k_ref, v_ref, kaug_ref, qaug_ref, dtab_ref, slope_ref, lp_ref, gn_ref,
                 *rest, tq, nq, qps, banded, lambda_init):
    o_ref, kp_ref, vp_ref, lhs_ref, acc_ref, s_ref, reach_ref = rest[-7:]
    step = pl.program_id(2)
    nk = nq
    assert nk % 2 == 0
    lane = lax.broadcasted_iota(jnp.int32, (tq, LANES), 1)

    def stacked_q(q):
        qs = q * (LOG2E * DIFF_HD ** -0.5)
        return jnp.concatenate([jnp.where(lane < DIFF_HD, qs, 0.0), jnp.where(lane >= DIFF_HD, qs, 0.0)],
                               axis=0).astype(BF16)

    def key_rows(kj):
        return pl.ds(pl.multiple_of(kj * tq, tq), tq)

    def diagonal_scores(q, kj):
        lhs = jnp.concatenate([stacked_q(q), jnp.zeros((2 * tq, LANES), BF16)], axis=1)
        return jnp.dot(lhs, kp_ref[kj], preferred_element_type=F32) + dtab_ref[0]

    def half_sums(x):
        half = (lax.broadcasted_iota(jnp.int32, (LANES, LANES), 0) < DIFF_HD) == (
            lax.broadcasted_iota(jnp.int32, (LANES, LANES), 1) < DIFF_HD)
        return jnp.dot(x.astype(BF16), jnp.where(half, 1.0, 0.0).astype(BF16), preferred_element_type=F32)

    def overall(reduce, x):
        return reduce(reduce(x, axis=0, keepdims=True), axis=1, keepdims=True)

    def reach_blocks():
        qs = (qf_ref[0] * (LOG2E * DIFF_HD ** -0.5)).astype(F32)
        ks = k_ref[0].astype(F32)
        bound = (1.0 + 2.0 * SUM_SLACK) * jnp.sqrt(
            overall(jnp.max, half_sums(qs * qs)) * overall(jnp.max, half_sums(ks * ks)))
        m_low = overall(jnp.min, half_sums(qs * ks) - SUM_SLACK * half_sums(jnp.abs(qs * ks)) - 0.1)
        blocks = ((bound - m_low - EXP2_ZERO_BELOW) / slope_ref[0, :, 0:1] - 1.0) * (1.0 / tq)
        return jnp.where(blocks < nk, jnp.floor(blocks) + 1.0, float(nk)).astype(jnp.int32)[0, 0]

    def new_head():
        for kb in range(nk):
            kp_ref[kb, 0:LANES, :] = k_ref[0, kb * tq:(kb + 1) * tq, :].T
            kp_ref[kb, LANES:2 * LANES, :] = kaug_ref[0, kb]
        vp_ref[:, 0:LANES] = v_ref[0]
        vp_ref[:, LANES:2 * LANES] = jnp.ones((vp_ref.shape[0], LANES), BF16)
        s_ref[0] = diagonal_scores(q_ref[0, 0:tq, :], 0)
        if banded:
            reach_ref[0] = reach_blocks()

    if nq == qps:
        new_head()
    else:
        pl.when(step == 0)(new_head)

    lp = lp_ref[...]
    lam = (jnp.exp(jnp.sum(lp[0:1] * lp[1:2], axis=-1, keepdims=True))
           - jnp.exp(jnp.sum(lp[2:3] * lp[3:4], axis=-1, keepdims=True)) + lambda_init)

    for j in range(qps):
        qi = step * qps + j
        rows_j = slice(j * tq, (j + 1) * tq)
        lhs, acc = lhs_ref.at[j], acc_ref.at[j]
        qa = jnp.concatenate([qaug_ref[0, rows_j, :], qaug_ref[0, rows_j, :]], axis=0)
        lhs[:, :, 0:LANES] = jnp.broadcast_to(stacked_q(q_ref[0, rows_j, :])[None], (2, 2 * tq, LANES))
        lhs[0, :, LANES:2 * LANES] = -qa
        lhs[1, :, LANES:2 * LANES] = qa

        if banded:
            reach = reach_ref[0]
            lo, hi = jnp.maximum(qi - reach, 0), jnp.minimum(qi + reach, nk - 1)
            odd = (hi - lo) % 2 == 0
            grow_hi = odd & (hi < nk - 1)
            lo, hi = jnp.where(odd & ~grow_hi, lo - 1, lo), jnp.where(grow_hi, hi + 1, hi)
            count = hi - lo + 1
        else:
            lo, count = 0, nk

        def accumulate(s, kj, m, acc=acc):
            m_new = jnp.max(s, axis=-1, keepdims=True)
            if m is not None:
                m_new = jnp.maximum(m, m_new)
            pv = jnp.dot(jnp.exp2(s - m_new).astype(BF16), vp_ref[key_rows(kj), :], preferred_element_type=F32)
            acc[...] = pv if m is None else acc[...] * jnp.exp2(m - m_new) + pv
            return m_new

        def block(n, qi=qi, lo=lo):
            return jnp.where(n == 0, qi, jnp.where(lo + n - 1 < qi, lo + n - 1, lo + n))

        def other_scores(n, qi=qi, lhs=lhs, block=block):
            kj = block(n)
            return jnp.dot(lhs[jnp.where(kj < qi, 0, 1)], kp_ref[kj], preferred_element_type=F32)

        def pair(t, m, accumulate=accumulate, block=block, other_scores=other_scores):
            s_ref[0] = other_scores(2 * t)
            m = accumulate(s_ref[1], block(2 * t - 1), m)
            s_ref[1] = other_scores(2 * t + 1)
            return accumulate(s_ref[0], block(2 * t), m)

        s_ref[1] = other_scores(1)
        m = accumulate(s_ref[0], qi, None)
        if banded:
            m = lax.fori_loop(1, count // 2, pair, m)
        else:
            m = lax.fori_loop(1, count // 2, pair, m, unroll=True)
        if j + 1 < qps:
            s_ref[0] = diagonal_scores(q_ref[0, (j + 1) * tq:(j + 2) * tq, :], qi + 1)
        else:
            s_ref[0] = diagonal_scores(qn_ref[0], jnp.minimum(qi + 1, nq - 1))
        accumulate(s_ref[1], block(count - 1), m)

        a1, a2 = acc[0:tq], acc[tq:2 * tq]
        o = a1[:, 0:LANES] / a1[:, LANES:2 * LANES] - lam * (a2[:, 0:LANES] / a2[:, LANES:2 * LANES])
        y = o * lax.rsqrt(jnp.mean(o * o, axis=-1, keepdims=True) + EPS)
        o_ref[0, rows_j, :] = (y * gn_ref[...]) * (1.0 - lambda_init)


def diff_attn(qkv, lam_params, g_norm, lambda_init, tq, qps, heads=(0, DIFF_HEADS), banded=False, prev=None):
    b, t, _ = qkv.shape
    nq = t // tq
    ns = nq // qps
    kaug, qaug, dtab, slope = _alibi_constants(t, tq)
    h_ = DIFF_HEADS
    h0, nh = heads[0], heads[1] - heads[0]
    operands = (qkv, qkv, qkv, qkv, qkv, kaug, qaug, dtab, slope, lam_params.astype(F32),
                g_norm.reshape(1, DIFF_DV).astype(F32))
    extra_specs, aliases = [], {}
    if prev is not None:
        extra_specs, aliases = [pl.BlockSpec(memory_space=pl.ANY)], {len(operands): 0}
        operands = operands + (prev,)
    return pl.pallas_call(
        functools.partial(_diff_kernel, tq=tq, nq=nq, qps=qps, banded=banded, lambda_init=lambda_init),
        out_shape=jax.ShapeDtypeStruct((b, t, DIFF_V), F32),
        grid=(b, nh, ns),
        in_specs=[pl.BlockSpec((1, qps * tq, LANES), lambda bi, h, si: (bi, si, h0 + h)),
                  pl.BlockSpec((1, tq, LANES), lambda bi, h, si: (bi, jnp.minimum((si + 1) * qps, nq - 1), h0 + h)),
                  pl.BlockSpec((1, t, LANES), lambda bi, h, si: (bi, 0, h0 + h)),
                  pl.BlockSpec((1, t, LANES), lambda bi, h, si: (bi, 0, h_ + h0 + h)),
                  pl.BlockSpec((1, t, LANES), lambda bi, h, si: (bi, 0, 2 * h_ + h0 + h)),
                  pl.BlockSpec((1, nq, LANES, tq), lambda bi, h, si: (h0 + h, 0, 0, 0)),
                  pl.BlockSpec((1, qps * tq, LANES), lambda bi, h, si: (h0 + h, si, 0)),
                  pl.BlockSpec((1, 2 * tq, tq), lambda bi, h, si: (h0 + h, 0, 0)),
                  pl.BlockSpec((1, 1, LANES), lambda bi, h, si: (h0 + h, 0, 0)),
                  pl.BlockSpec((4, DIFF_HD), lambda bi, h, si: (0, 0)),
                  pl.BlockSpec((1, DIFF_DV), lambda bi, h, si: (0, 0))] + extra_specs,
        out_specs=pl.BlockSpec((1, qps * tq, LANES), lambda bi, h, si: (bi, si, h0 + h)),
        input_output_aliases=aliases,
        scratch_shapes=[pltpu.VMEM((nq, 2 * LANES, tq), BF16),
                        pltpu.VMEM((t, 2 * LANES), BF16),
                        pltpu.VMEM((qps, 2, 2 * tq, 2 * LANES), BF16),
                        pltpu.VMEM((qps, 2 * tq, 2 * LANES), F32),
                        pltpu.VMEM((2, 2 * tq, tq), F32),
                        pltpu.SMEM((1,), jnp.int32)],
        compiler_params=_cparams(("parallel", "parallel", "arbitrary")),
        name="diff_attn_banded" if banded else "diff_attn",
    )(*operands)


def _cross_kernel(q_ref, kv_ref, o_ref):
    for h in range(CROSS_HEADS):
        qh = q_ref[0, :, h * CROSS_HD:(h + 1) * CROSS_HD]
        kh = kv_ref[0, :, h * CROSS_HD:(h + 1) * CROSS_HD]
        vh = kv_ref[0, :, CROSS_Q + h * CROSS_HD:CROSS_Q + (h + 1) * CROSS_HD]
        s = lax.dot_general(qh, kh, _NT, preferred_element_type=F32) * (CROSS_HD ** -0.5)
        p = jnp.exp(s - jnp.max(s, axis=-1, keepdims=True))
        l = jnp.sum(p, axis=-1, keepdims=True)
        o = jnp.dot(p.astype(BF16), vh, preferred_element_type=F32)
        o_ref[0, :, h * CROSS_HD:(h + 1) * CROSS_HD] = o / l


def cross_attn(q, kv, tq):
    b, t, _ = q.shape
    return pl.pallas_call(
        _cross_kernel,
        out_shape=jax.ShapeDtypeStruct((b, t, CROSS_Q), F32),
        grid=(b, t // tq),
        in_specs=[pl.BlockSpec((1, tq, CROSS_Q), lambda bi, i: (bi, i, 0)),
                  pl.BlockSpec((1, N_MEM, 2 * CROSS_Q), lambda bi, i: (bi, 0, 0))],
        out_specs=pl.BlockSpec((1, tq, CROSS_Q), lambda bi, i: (bi, i, 0)),
        compiler_params=_cparams(("parallel", "parallel")),
        name="cross_attn",
    )(q, kv)


def _silu(z):
    return z / (1.0 + jnp.exp(-z))


def _sigmoid(z):
    return 1.0 / (1.0 + jnp.exp(-z))


def _merge_kernel(x_ref, of_ref, ob_ref, od_ref, oc_ref, ng_ref, wg_ref, gn_ref, wb_ref, wo_ref, fg_ref, o_ref,
                  *, final):
    x = x_ref[...]
    h = ((x * lax.rsqrt(jnp.mean(x * x, axis=-1, keepdims=True) + EPS)) * ng_ref[...]).astype(BF16)
    og = of_ref[...] + ob_ref[...]
    parts = []
    for hd in range(GLA_HEADS):
        oh = og[:, hd * GLA_DV:(hd + 1) * GLA_DV]
        parts.append(oh * lax.rsqrt(jnp.mean(oh * oh, axis=-1, keepdims=True) + EPS) * gn_ref[...])
    branches = (jnp.concatenate(parts, axis=1), od_ref[...], oc_ref[...])
    merged = None
    for i in range(N_BRANCH):
        z = jnp.dot(h, wg_ref[:, i * BRANCH_W:(i + 1) * BRANCH_W], preferred_element_type=F32)
        a = (branches[i] * _silu(z)).astype(BF16)
        y = jnp.dot(a, wb_ref[i], preferred_element_type=F32)
        c0 = N_BRANCH * BRANCH_W + i * D_MODEL
        gate = _sigmoid(jnp.dot(h, wg_ref[:, c0:c0 + D_MODEL], preferred_element_type=F32))
        merged = gate * y if merged is None else merged + gate * y
    x = x + jnp.dot(merged.astype(BF16), wo_ref[...], preferred_element_type=F32)
    if final:
        x = (x * lax.rsqrt(jnp.mean(x * x, axis=-1, keepdims=True) + EPS)) * fg_ref[...]
    o_ref[...] = x


def merge(x2d, of, ob, od, oc, norm_g, w_gates, gla_norm_g, wb, wo, final_g, final, tm):
    m, d = x2d.shape
    row = lambda w: pl.BlockSpec((tm, w), lambda i: (i, 0))
    return pl.pallas_call(
        functools.partial(_merge_kernel, final=final),
        out_shape=jax.ShapeDtypeStruct((m, d), F32),
        grid=(m // tm,),
        in_specs=[row(d), row(BRANCH_W), row(BRANCH_W), row(BRANCH_W), row(BRANCH_W),
                  _resident((1, d)), _resident(w_gates.shape), _resident((1, GLA_DV)),
                  _resident(wb.shape), _resident(wo.shape), _resident((1, d))],
        out_specs=row(d),
        compiler_params=_cparams(("parallel",)),
        name="merge_final" if final else "merge",
    )(x2d, of, ob, od, oc, norm_g.reshape(1, d).astype(F32), w_gates,
      gla_norm_g.reshape(1, GLA_DV).astype(F32), wb, wo, final_g.reshape(1, d).astype(F32))


def _split_w_in(w_in_l):
    offs = np.concatenate([[0], np.cumsum(np.array(SPLITS))])
    col = lambda i: w_in_l[:, int(offs[i]):int(offs[i + 1])]
    g_q, g_k, g_v, g_lr, z_a, d_q, d_k, d_v, z_b, c_q, z_c, gate_in = (col(i) for i in range(len(SPLITS)))
    lr_pad = jnp.pad(g_lr, ((0, 0), (0, GLR_PAD - 2 * GLA_RANK)))
    w_branch_in = jnp.concatenate([g_q, g_k, g_v, lr_pad, d_q, d_k, d_v, c_q], axis=1).astype(BF16)
    w_gates = jnp.concatenate([z_a, z_b, z_c, gate_in], axis=1).astype(BF16)
    return w_branch_in, w_gates


def _gate_weights(w2_l, b2_l, direction):
    w2pad = jnp.zeros((GLR_PAD, GLA_Q), F32)
    w2pad = w2pad.at[direction * GLA_RANK:(direction + 1) * GLA_RANK].set(w2_l[direction].astype(F32))
    return w2pad.astype(BF16), b2_l[direction].reshape(1, GLA_Q).astype(F32)


def _tiles(t):
    tq = min(t // 2, 512)
    qps = 2 if t // tq <= 4 else 1
    band_heads = DIFF_HEADS // 2 if t // tq > 4 else 0
    return dict(tm_in=1024, tm_out=256, tb=min(t, 1024), tq=tq, qps=qps, band_heads=band_heads, tc=min(t, 2048))


def _trunk(x, mem, norm_g, w_in, gla_gate_w2, gla_gate_b, gla_norm_g, diff_lambda, diff_norm_g,
           mem_norm_g, w_mem_kv, w_branch, w_out, final_norm_g):
    b, t, d = x.shape
    tl = _tiles(t)
    x2d = x.reshape(b * t, d)
    mem2d = mem.reshape(b * N_MEM, d)
    for l in range(DEPTH):
        lambda_init = 0.8 - 0.6 * math.exp(-0.3 * l)
        w_branch_in, w_gates = _split_w_in(w_in[l])
        g_in, qkv, cq = branch_inputs(x2d, norm_g[l], w_branch_in, min(tl["tm_in"], b * t))
        g_in, qkv, cq = g_in.reshape(b, t, GLA_IN), qkv.reshape(b, t, 3 * DIFF_QK), cq.reshape(b, t, CROSS_Q)
        kv = norm_matmul(mem2d, mem_norm_g[l], w_mem_kv[l].astype(BF16), BF16, N_MEM, 2 * CROSS_Q)
        kv = kv.reshape(b, N_MEM, 2 * CROSS_Q)
        o_f = gla(g_in, *_gate_weights(gla_gate_w2[l], gla_gate_b[l], 0), False, tl["tb"])
        o_b = gla(g_in, *_gate_weights(gla_gate_w2[l], gla_gate_b[l], 1), True, tl["tb"])
        if tl["band_heads"]:
            hb = tl["band_heads"]
            o_d = diff_attn(qkv, diff_lambda[l], diff_norm_g[l], lambda_init, tl["tq"], 1, (0, hb), True)
            o_d = diff_attn(qkv, diff_lambda[l], diff_norm_g[l], lambda_init, tl["tq"], 1, (hb, DIFF_HEADS),
                            prev=o_d)
        else:
            o_d = diff_attn(qkv, diff_lambda[l], diff_norm_g[l], lambda_init, tl["tq"], tl["qps"])
        o_c = cross_attn(cq, kv, tl["tc"])
        x2d = merge(x2d, o_f.reshape(b * t, GLA_V), o_b.reshape(b * t, GLA_V), o_d.reshape(b * t, DIFF_V),
                    o_c.reshape(b * t, CROSS_Q), norm_g[l], w_gates, gla_norm_g[l], w_branch[l].astype(BF16),
                    w_out[l].astype(BF16), final_norm_g, l == DEPTH - 1, tl["tm_out"])
    return x2d.reshape(b, t, d)


def kernel(x_prompt, x_sample, mem_prompt, mem_sample, norm_g, w_in, gla_gate_w2, gla_gate_b, gla_norm_g,
           diff_lambda, diff_norm_g, mem_norm_g, w_mem_kv, w_branch, w_out, final_norm_g):
    params = (norm_g, w_in, gla_gate_w2, gla_gate_b, gla_norm_g, diff_lambda, diff_norm_g, mem_norm_g,
              w_mem_kv, w_branch, w_out, final_norm_g)
    return (_trunk(x_prompt, mem_prompt, *params), _trunk(x_sample, mem_sample, *params))
```

```python
import functools
import math

import numpy as np
import jax
import jax.numpy as jnp
from jax import lax
from jax.experimental import pallas as pl
from jax.experimental.pallas import tpu as pltpu

F32 = jnp.float32
BF16 = jnp.bfloat16

D_MODEL = 1024
DEPTH = 4
N_MEM = 256
BRANCH_W = 512
N_BRANCH = 3
EPS = 1e-6
GLA_HEADS = 4
GLA_DK = 64
GLA_DV = 128
GLA_RANK = 16
GLA_TAU = 16.0
GLA_Q = GLA_HEADS * GLA_DK
GLA_V = GLA_HEADS * GLA_DV
DIFF_HEADS = 4
DIFF_HD = 64
DIFF_DV = 2 * DIFF_HD
DIFF_QK = DIFF_HEADS * 2 * DIFF_HD
DIFF_V = DIFF_HEADS * DIFF_DV
CROSS_HEADS = 4
CROSS_HD = 128
CROSS_Q = CROSS_HEADS * CROSS_HD
SPLITS = (GLA_Q, GLA_Q, GLA_V, 2 * GLA_RANK, BRANCH_W,
          DIFF_QK, DIFF_QK, DIFF_V, BRANCH_W,
          CROSS_Q, BRANCH_W, N_BRANCH * D_MODEL)

LANES = 128
LOG2E = 1.4426950408889634
EXP2_ZERO_BELOW = -150.0
SUM_SLACK = 0.01
VMEM_LIMIT = 56 * 1024 * 1024
GLA_CHUNK = 128
GLA_LEVELS = 7
GLR_PAD = LANES
GLA_IN = 2 * GLA_Q + GLA_V + GLR_PAD

_NT = (((1,), (1,)), ((), ()))
_TN = (((0,), (0,)), ((), ()))


def _cparams(sem):
    return pltpu.CompilerParams(dimension_semantics=sem, vmem_limit_bytes=VMEM_LIMIT)


def _resident(shape):
    return pl.BlockSpec(shape, lambda *_: (0,) * len(shape), pipeline_mode=pl.Buffered(1))


def _norm_matmul_kernel(x_ref, g_ref, w_ref, o_ref, h_ref):
    @pl.when(pl.program_id(1) == 0)
    def _():
        x = x_ref[...]
        y = x * lax.rsqrt(jnp.mean(x * x, axis=-1, keepdims=True) + EPS)
        h_ref[...] = (y * g_ref[...]).astype(BF16)

    o_ref[...] = jnp.dot(h_ref[...], w_ref[...], preferred_element_type=F32).astype(o_ref.dtype)


def norm_matmul(x2d, g, w_bf16, out_dtype, tm, tn):
    m, d = x2d.shape
    n = w_bf16.shape[1]
    return pl.pallas_call(
        _norm_matmul_kernel,
        out_shape=jax.ShapeDtypeStruct((m, n), out_dtype),
        grid=(m // tm, n // tn),
        in_specs=[pl.BlockSpec((tm, d), lambda i, j: (i, 0)),
                  pl.BlockSpec((1, d), lambda i, j: (0, 0)),
                  pl.BlockSpec((d, tn), lambda i, j: (0, j))],
        out_specs=pl.BlockSpec((tm, tn), lambda i, j: (i, j)),
        scratch_shapes=[pltpu.VMEM((tm, d), BF16)],
        compiler_params=_cparams(("parallel", "arbitrary")),
        name="norm_matmul",
    )(x2d, g.reshape(1, d).astype(F32), w_bf16)


def _proj_kernel(x_ref, g_ref, w_ref, gla_ref, qkv_ref, cq_ref):
    x = x_ref[...]
    h = ((x * lax.rsqrt(jnp.mean(x * x, axis=-1, keepdims=True) + EPS)) * g_ref[...]).astype(BF16)
    c1, c2 = GLA_IN, GLA_IN + 3 * DIFF_QK
    gla_ref[...] = jnp.dot(h, w_ref[:, 0:c1], preferred_element_type=F32)
    qkv_ref[...] = jnp.dot(h, w_ref[:, c1:c2], preferred_element_type=F32).astype(BF16)
    cq_ref[...] = jnp.dot(h, w_ref[:, c2:c2 + CROSS_Q], preferred_element_type=F32).astype(BF16)


def branch_inputs(x2d, g, w_bf16, tm):
    m, d = x2d.shape
    row = lambda w: pl.BlockSpec((tm, w), lambda i: (i, 0))
    return pl.pallas_call(
        _proj_kernel,
        out_shape=(jax.ShapeDtypeStruct((m, GLA_IN), F32),
                   jax.ShapeDtypeStruct((m, 3 * DIFF_QK), BF16),
                   jax.ShapeDtypeStruct((m, CROSS_Q), BF16)),
        grid=(m // tm,),
        in_specs=[row(d), _resident((1, d)), _resident(w_bf16.shape)],
        out_specs=(row(GLA_IN), row(3 * DIFF_QK), row(CROSS_Q)),
        compiler_params=_cparams(("parallel",)),
        name="branch_inputs",
    )(x2d, g.reshape(1, d).astype(F32), w_bf16)


def _gla_constants(reverse):
    c = GLA_CHUNK
    t = np.arange(c)[:, None]
    u = np.arange(c)[None, :]
    start, end, total = (u <= t), (u > t), np.ones((c, c), bool)
    levels, masks = [], [(t == u)]
    for lvl in range(1, GLA_LEVELS + 1):
        bs, half = 1 << lvl, 1 << (lvl - 1)
        boundary = (t // bs) * bs + half - 1
        right = (t % bs) >= half
        levels.append(np.where(right, (u > boundary) & (u <= t), (u > t) & (u <= boundary)))
        masks.append(((u // bs) == (t // bs)) & right & ((u % bs) < half))
    flip = (lambda m: m[::-1, ::-1]) if reverse else (lambda m: m)
    mall_t = np.concatenate([flip(m).T for m in [start, end] + levels + [total]], axis=1).astype(np.float32)
    mask = np.stack([flip(m) for m in masks], axis=0).astype(np.float32)
    mask = np.concatenate([mask, mask], axis=-1)
    return jnp.asarray(mall_t, BF16), jnp.asarray(np.ascontiguousarray(mask), F32)


def _log_decay(logits):
    return (jnp.minimum(logits, 0.0) - jnp.log1p(jnp.exp(-jnp.abs(logits)))) * (1.0 / GLA_TAU)


def _gla_kernel(g_ref, mt_ref, mask_ref, w2t_ref, b2t_ref, o_ref, st_ref, *, reverse, n_chunks):
    c = GLA_CHUNK

    @pl.when(pl.program_id(1) == 0)
    def _():
        st_ref[...] = jnp.zeros_like(st_ref)

    lane_v = lax.broadcasted_iota(jnp.int32, (c, 2 * GLA_DV), 1)
    vlo, vhi = lane_v < GLA_DV, lane_v >= GLA_DV
    drow = lax.broadcasted_iota(jnp.int32, (LANES, c), 0)
    dlo, dhi = drow < GLA_DK, drow >= GLA_DK
    state_mask = ((lax.broadcasted_iota(jnp.int32, (LANES, 2 * GLA_DV), 0) < GLA_DK)
                  == (lax.broadcasted_iota(jnp.int32, (LANES, 2 * GLA_DV), 1) < GLA_DV))

    def chunk(ci, carry):
        idx = (n_chunks - 1 - ci) if reverse else ci
        rows = pl.ds(pl.multiple_of(idx * c, c), c)
        glr_t = g_ref[0, rows, 2 * GLA_Q + GLA_V:GLA_IN].T.astype(BF16)
        log_at = _log_decay(jnp.dot(w2t_ref[...], glr_t, preferred_element_type=F32) + b2t_ref[...])
        e_all = jnp.exp(jnp.dot(log_at.astype(BF16), mt_ref[...], preferred_element_type=F32))
        for p in range(GLA_HEADS // 2):
            q_t = g_ref[0, rows, p * LANES:(p + 1) * LANES].T * (GLA_DK ** -0.5)
            k_t = g_ref[0, rows, GLA_Q + p * LANES:GLA_Q + (p + 1) * LANES].T
            v = g_ref[0, rows, 2 * GLA_Q + p * 2 * GLA_DV:2 * GLA_Q + (p + 1) * 2 * GLA_DV]
            v2 = jnp.concatenate([jnp.where(vlo, v, 0.0), jnp.where(vhi, v, 0.0)], axis=0).astype(BF16)
            e = lambda blk: e_all[p * LANES:(p + 1) * LANES, blk * c:(blk + 1) * c]
            st = st_ref[p]
            o = lax.dot_general((q_t * e(0)).astype(BF16), st.astype(BF16), _TN, preferred_element_type=F32)
            att = jnp.zeros((c, 2 * c), F32)
            for lvl in range(GLA_LEVELS + 1):
                qs, ks = (q_t, k_t) if lvl == 0 else (q_t * e(lvl + 1), k_t * e(lvl + 1))
                k2 = jnp.concatenate([jnp.where(dlo, ks, 0.0), jnp.where(dhi, ks, 0.0)], axis=1)
                a = lax.dot_general(qs.astype(BF16), k2.astype(BF16), _TN, preferred_element_type=F32)
                att = att + a * mask_ref[lvl]
            o = o + jnp.dot(att.astype(BF16), v2, preferred_element_type=F32)
            o_ref[0, rows, p * 2 * GLA_DV:(p + 1) * 2 * GLA_DV] = o
            upd = jnp.dot((k_t * e(1)).astype(BF16), v.astype(BF16), preferred_element_type=F32)
            decay = e(GLA_LEVELS + 2)
            st_ref[p] = st * jnp.concatenate([decay, decay], axis=1) + jnp.where(state_mask, upd, 0.0)
        return carry

    lax.fori_loop(0, n_chunks, chunk, 0, unroll=True)


def gla(g_in, w2pad, b2, reverse, tb):
    b, t, _ = g_in.shape
    nblk = t // tb
    mall_t, mask = _gla_constants(reverse)
    b2t = jnp.broadcast_to(b2.reshape(GLA_Q, 1), (GLA_Q, GLA_CHUNK))
    tmap = (lambda bi, i: (bi, nblk - 1 - i, 0)) if reverse else (lambda bi, i: (bi, i, 0))
    consts = (mall_t, mask, w2pad.T, b2t)
    return pl.pallas_call(
        functools.partial(_gla_kernel, reverse=reverse, n_chunks=tb // GLA_CHUNK),
        out_shape=jax.ShapeDtypeStruct((b, t, GLA_V), F32),
        grid=(b, nblk),
        in_specs=[pl.BlockSpec((1, tb, GLA_IN), tmap)] + [_resident(a.shape) for a in consts],
        out_specs=pl.BlockSpec((1, tb, GLA_V), tmap),
        scratch_shapes=[pltpu.VMEM((GLA_HEADS // 2, LANES, 2 * GLA_DV), F32)],
        compiler_params=_cparams(("parallel", "arbitrary")),
        name="gla_bwd" if reverse else "gla_fwd",
    )(g_in, *consts)


def _bf16_terms(x, n=3):
    terms, rest = [], np.asarray(x, np.float64)
    for _ in range(n):
        bits = rest.astype(np.float32).view(np.uint32)
        bits = (bits + np.uint32(0x7FFF) + ((bits >> np.uint32(16)) & np.uint32(1))) & np.uint32(0xFFFF0000)
        term = bits.view(np.float32)
        terms.append(term)
        rest = rest - term.astype(np.float64)
    return terms


def _alibi_constants(t, tq):
    slopes = LOG2E * 2.0 ** (-8.0 * (np.arange(DIFF_HEADS, dtype=np.float64) + 1.0) / DIFF_HEADS)
    pos = np.arange(t)
    parts = ((pos // LANES) * LANES, pos % LANES)
    n = 3 * len(parts)
    kaug = np.zeros((DIFF_HEADS, t, LANES), np.float32)
    qaug = np.zeros((DIFF_HEADS, t, LANES), np.float32)
    kaug[:, :, n:2 * n] = 1.0
    qaug[:, :, 0:n] = 1.0
    for h in range(DIFF_HEADS):
        for j, part in enumerate(parts):
            for i, term in enumerate(_bf16_terms(slopes[h] * part)):
                kaug[h, :, 3 * j + i] = -term
                qaug[h, :, n + 3 * j + i] = term
    i = np.arange(tq)
    dtab = -slopes[:, None, None] * np.abs(i[:, None] - i[None, :])[None]
    dtab = np.concatenate([dtab, dtab], axis=1)
    kaug_t = kaug.reshape(DIFF_HEADS, t // tq, tq, LANES).transpose(0, 1, 3, 2)
    slope_rows = np.broadcast_to(slopes[:, None, None], (DIFF_HEADS, 1, LANES))
    return jnp.asarray(kaug_t, BF16), jnp.asarray(qaug, BF16), jnp.asarray(dtab, F32), jnp.asarray(slope_rows, F32)


def _diff_kernel(q_ref, qn_ref, qf_ref, k_ref, v_ref, kaug_ref, qaug_ref, dtab_ref, slope_ref, lp_ref, gn_ref,
                 *rest, tq, nq, qps, banded, lambda_init):
    o_ref, kp_ref, vp_ref, lhs_ref, acc_ref, s_ref, reach_ref = rest[-7:]
    step = pl.program_id(2)
    nk = nq
    assert nk % 2 == 0
    lane = lax.broadcasted_iota(jnp.int32, (tq, LANES), 1)

    def stacked_q(q):
        qs = q * (LOG2E * DIFF_HD ** -0.5)
        return jnp.concatenate([jnp.where(lane < DIFF_HD, qs, 0.0), jnp.where(lane >= DIFF_HD, qs, 0.0)],
                               axis=0).astype(BF16)

    def key_rows(kj):
        return pl.ds(pl.multiple_of(kj * tq, tq), tq)

    def diagonal_scores(q, kj):
        lhs = jnp.concatenate([stacked_q(q), jnp.zeros((2 * tq, LANES), BF16)], axis=1)
        return jnp.dot(lhs, kp_ref[kj], preferred_element_type=F32) + dtab_ref[0]

    def half_sums(x):
        half = (lax.broadcasted_iota(jnp.int32, (LANES, LANES), 0) < DIFF_HD) == (
            lax.broadcasted_iota(jnp.int32, (LANES, LANES), 1) < DIFF_HD)
        return jnp.dot(x.astype(BF16), jnp.where(half, 1.0, 0.0).astype(BF16), preferred_element_type=F32)

    def overall(reduce, x):
        return reduce(reduce(x, axis=0, keepdims=True), axis=1, keepdims=True)

    def reach_blocks():
        qs = (qf_ref[0] * (LOG2E * DIFF_HD ** -0.5)).astype(F32)
        ks = k_ref[0].astype(F32)
        bound = (1.0 + 2.0 * SUM_SLACK) * jnp.sqrt(
            overall(jnp.max, half_sums(qs * qs)) * overall(jnp.max, half_sums(ks * ks)))
        m_low = overall(jnp.min, half_sums(qs * ks) - SUM_SLACK * half_sums(jnp.abs(qs * ks)) - 0.1)
        blocks = ((bound - m_low - EXP2_ZERO_BELOW) / slope_ref[0, :, 0:1] - 1.0) * (1.0 / tq)
        return jnp.where(blocks < nk, jnp.floor(blocks) + 1.0, float(nk)).astype(jnp.int32)[0, 0]

    def new_head():
        for kb in range(nk):
            kp_ref[kb, 0:LANES, :] = k_ref[0, kb * tq:(kb + 1) * tq, :].T
            kp_ref[kb, LANES:2 * LANES, :] = kaug_ref[0, kb]
        vp_ref[:, 0:LANES] = v_ref[0]
        vp_ref[:, LANES:2 * LANES] = jnp.ones((vp_ref.shape[0], LANES), BF16)
        s_ref[0] = diagonal_scores(q_ref[0, 0:tq, :], 0)
        if banded:
            reach_ref[0] = reach_blocks()

    if nq == qps:
        new_head()
    else:
        pl.when(step == 0)(new_head)

    lp = lp_ref[...]
    lam = (jnp.exp(jnp.sum(lp[0:1] * lp[1:2], axis=-1, keepdims=True))
           - jnp.exp(jnp.sum(lp[2:3] * lp[3:4], axis=-1, keepdims=True)) + lambda_init)

    for j in range(qps):
        qi = step * qps + j
        rows_j = slice(j * tq, (j + 1) * tq)
        lhs, acc = lhs_ref.at[j], acc_ref.at[j]
        qa = jnp.concatenate([qaug_ref[0, rows_j, :], qaug_ref[0, rows_j, :]], axis=0)
        lhs[:, :, 0:LANES] = jnp.broadcast_to(stacked_q(q_ref[0, rows_j, :])[None], (2, 2 * tq, LANES))
        lhs[0, :, LANES:2 * LANES] = -qa
        lhs[1, :, LANES:2 * LANES] = qa

        if banded:
            reach = reach_ref[0]
            lo, hi = jnp.maximum(qi - reach, 0), jnp.minimum(qi + reach, nk - 1)
            odd = (hi - lo) % 2 == 0
            grow_hi = odd & (hi < nk - 1)
            lo, hi = jnp.where(odd & ~grow_hi, lo - 1, lo), jnp.where(grow_hi, hi + 1, hi)
            count = hi - lo + 1
        else:
            lo, count = 0, nk

        def accumulate(s, kj, m, acc=acc):
            m_new = jnp.max(s, axis=-1, keepdims=True)
            if m is not None:
                m_new = jnp.maximum(m, m_new)
            pv = jnp.dot(jnp.exp2(s - m_new).astype(BF16), vp_ref[key_rows(kj), :], preferred_element_type=F32)
            acc[...] = pv if m is None else acc[...] * jnp.exp2(m - m_new) + pv
            return m_new

        def block(n, qi=qi, lo=lo):
            return jnp.where(n == 0, qi, jnp.where(lo + n - 1 < qi, lo + n - 1, lo + n))

        def other_scores(n, qi=qi, lhs=lhs, block=block):
            kj = block(n)
            return jnp.dot(lhs[jnp.where(kj < qi, 0, 1)], kp_ref[kj], preferred_element_type=F32)

        def pair(t, m, accumulate=accumulate, block=block, other_scores=other_scores):
            s_ref[0] = other_scores(2 * t)
            m = accumulate(s_ref[1], block(2 * t - 1), m)
            s_ref[1] = other_scores(2 * t + 1)
            return accumulate(s_ref[0], block(2 * t), m)

        s_ref[1] = other_scores(1)
        m = accumulate(s_ref[0], qi, None)
        if banded:
            m = lax.fori_loop(1, count // 2, pair, m)
        else:
            m = lax.fori_loop(1, count // 2, pair, m, unroll=True)
        if j + 1 < qps:
            s_ref[0] = diagonal_scores(q_ref[0, (j + 1) * tq:(j + 2) * tq, :], qi + 1)
        else:
            s_ref[0] = diagonal_scores(qn_ref[0], jnp.minimum(qi + 1, nq - 1))
        accumulate(s_ref[1], block(count - 1), m)

        a1, a2 = acc[0:tq], acc[tq:2 * tq]
        o = a1[:, 0:LANES] / a1[:, LANES:2 * LANES] - lam * (a2[:, 0:LANES] / a2[:, LANES:2 * LANES])
        y = o * lax.rsqrt(jnp.mean(o * o, axis=-1, keepdims=True) + EPS)
        o_ref[0, rows_j, :] = (y * gn_ref[...]) * (1.0 - lambda_init)


def diff_attn(qkv, lam_params, g_norm, lambda_init, tq, qps, heads=(0, DIFF_HEADS), banded=False, prev=None):
    b, t, _ = qkv.shape
    nq = t // tq
    ns = nq // qps
    kaug, qaug, dtab, slope = _alibi_constants(t, tq)
    h_ = DIFF_HEADS
    h0, nh = heads[0], heads[1] - heads[0]
    operands = (qkv, qkv, qkv, qkv, qkv, kaug, qaug, dtab, slope, lam_params.astype(F32),
                g_norm.reshape(1, DIFF_DV).astype(F32))
    extra_specs, aliases = [], {}
    if prev is not None:
        extra_specs, aliases = [pl.BlockSpec(memory_space=pl.ANY)], {len(operands): 0}
        operands = operands + (prev,)
    return pl.pallas_call(
        functools.partial(_diff_kernel, tq=tq, nq=nq, qps=qps, banded=banded, lambda_init=lambda_init),
        out_shape=jax.ShapeDtypeStruct((b, t, DIFF_V), F32),
        grid=(b, nh, ns),
        in_specs=[pl.BlockSpec((1, qps * tq, LANES), lambda bi, h, si: (bi, si, h0 + h)),
                  pl.BlockSpec((1, tq, LANES), lambda bi, h, si: (bi, jnp.minimum((si + 1) * qps, nq - 1), h0 + h)),
                  pl.BlockSpec((1, t, LANES), lambda bi, h, si: (bi, 0, h0 + h)),
                  pl.BlockSpec((1, t, LANES), lambda bi, h, si: (bi, 0, h_ + h0 + h)),
                  pl.BlockSpec((1, t, LANES), lambda bi, h, si: (bi, 0, 2 * h_ + h0 + h)),
                  pl.BlockSpec((1, nq, LANES, tq), lambda bi, h, si: (h0 + h, 0, 0, 0)),
                  pl.BlockSpec((1, qps * tq, LANES), lambda bi, h, si: (h0 + h, si, 0)),
                  pl.BlockSpec((1, 2 * tq, tq), lambda bi, h, si: (h0 + h, 0, 0)),
                  pl.BlockSpec((1, 1, LANES), lambda bi, h, si: (h0 + h, 0, 0)),
                  pl.BlockSpec((4, DIFF_HD), lambda bi, h, si: (0, 0)),
                  pl.BlockSpec((1, DIFF_DV), lambda bi, h, si: (0, 0))] + extra_specs,
        out_specs=pl.BlockSpec((1, qps * tq, LANES), lambda bi, h, si: (bi, si, h0 + h)),
        input_output_aliases=aliases,
        scratch_shapes=[pltpu.VMEM((nq, 2 * LANES, tq), BF16),
                        pltpu.VMEM((t, 2 * LANES), BF16),
                        pltpu.VMEM((qps, 2, 2 * tq, 2 * LANES), BF16),
                        pltpu.VMEM((qps, 2 * tq, 2 * LANES), F32),
                        pltpu.VMEM((2, 2 * tq, tq), F32),
                        pltpu.SMEM((1,), jnp.int32)],
        compiler_params=_cparams(("parallel", "parallel", "arbitrary")),
        name="diff_attn_banded" if banded else "diff_attn",
    )(*operands)


def _cross_kernel(q_ref, kv_ref, o_ref):
    for h in range(CROSS_HEADS):
        qh = q_ref[0, :, h * CROSS_HD:(h + 1) * CROSS_HD]
        kh = kv_ref[0, :, h * CROSS_HD:(h + 1) * CROSS_HD]
        vh = kv_ref[0, :, CROSS_Q + h * CROSS_HD:CROSS_Q + (h + 1) * CROSS_HD]
        s = lax.dot_general(qh, kh, _NT, preferred_element_type=F32) * (CROSS_HD ** -0.5)
        p = jnp.exp(s - jnp.max(s, axis=-1, keepdims=True))
        l = jnp.sum(p, axis=-1, keepdims=True)
        o = jnp.dot(p.astype(BF16), vh, preferred_element_type=F32)
        o_ref[0, :, h * CROSS_HD:(h + 1) * CROSS_HD] = o / l


def cross_attn(q, kv, tq):
    b, t, _ = q.shape
    return pl.pallas_call(
        _cross_kernel,
        out_shape=jax.ShapeDtypeStruct((b, t, CROSS_Q), F32),
        grid=(b, t // tq),
        in_specs=[pl.BlockSpec((1, tq, CROSS_Q), lambda bi, i: (bi, i, 0)),
                  pl.BlockSpec((1, N_MEM, 2 * CROSS_Q), lambda bi, i: (bi, 0, 0))],
        out_specs=pl.BlockSpec((1, tq, CROSS_Q), lambda bi, i: (bi, i, 0)),
        compiler_params=_cparams(("parallel", "parallel")),
        name="cross_attn",
    )(q, kv)


def _silu(z):
    return z / (1.0 + jnp.exp(-z))


def _sigmoid(z):
    return 1.0 / (1.0 + jnp.exp(-z))


def _merge_kernel(x_ref, of_ref, ob_ref, od_ref, oc_ref, ng_ref, wg_ref, gn_ref, wb_ref, wo_ref, fg_ref, o_ref,
                  *, final):
    x = x_ref[...]
    h = ((x * lax.rsqrt(jnp.mean(x * x, axis=-1, keepdims=True) + EPS)) * ng_ref[...]).astype(BF16)
    og = of_ref[...] + ob_ref[...]
    parts = []
    for hd in range(GLA_HEADS):
        oh = og[:, hd * GLA_DV:(hd + 1) * GLA_DV]
        parts.append(oh * lax.rsqrt(jnp.mean(oh * oh, axis=-1, keepdims=True) + EPS) * gn_ref[...])
    branches = (jnp.concatenate(parts, axis=1), od_ref[...], oc_ref[...])
    merged = None
    for i in range(N_BRANCH):
        z = jnp.dot(h, wg_ref[:, i * BRANCH_W:(i + 1) * BRANCH_W], preferred_element_type=F32)
        a = (branches[i] * _silu(z)).astype(BF16)
        y = jnp.dot(a, wb_ref[i], preferred_element_type=F32)
        c0 = N_BRANCH * BRANCH_W + i * D_MODEL
        gate = _sigmoid(jnp.dot(h, wg_ref[:, c0:c0 + D_MODEL], preferred_element_type=F32))
        merged = gate * y if merged is None else merged + gate * y
    x = x + jnp.dot(merged.astype(BF16), wo_ref[...], preferred_element_type=F32)
    if final:
        x = (x * lax.rsqrt(jnp.mean(x * x, axis=-1, keepdims=True) + EPS)) * fg_ref[...]
    o_ref[...] = x


def merge(x2d, of, ob, od, oc, norm_g, w_gates, gla_norm_g, wb, wo, final_g, final, tm):
    m, d = x2d.shape
    row = lambda w: pl.BlockSpec((tm, w), lambda i: (i, 0))
    return pl.pallas_call(
        functools.partial(_merge_kernel, final=final),
        out_shape=jax.ShapeDtypeStruct((m, d), F32),
        grid=(m // tm,),
        in_specs=[row(d), row(BRANCH_W), row(BRANCH_W), row(BRANCH_W), row(BRANCH_W),
                  _resident((1, d)), _resident(w_gates.shape), _resident((1, GLA_DV)),
                  _resident(wb.shape), _resident(wo.shape), _resident((1, d))],
        out_specs=row(d),
        compiler_params=_cparams(("parallel",)),
        name="merge_final" if final else "merge",
    )(x2d, of, ob, od, oc, norm_g.reshape(1, d).astype(F32), w_gates,
      gla_norm_g.reshape(1, GLA_DV).astype(F32), wb, wo, final_g.reshape(1, d).astype(F32))


def _split_w_in(w_in_l):
    offs = np.concatenate([[0], np.cumsum(np.array(SPLITS))])
    col = lambda i: w_in_l[:, int(offs[i]):int(offs[i + 1])]
    g_q, g_k, g_v, g_lr, z_a, d_q, d_k, d_v, z_b, c_q, z_c, gate_in = (col(i) for i in range(len(SPLITS)))
    lr_pad = jnp.pad(g_lr, ((0, 0), (0, GLR_PAD - 2 * GLA_RANK)))
    w_branch_in = jnp.concatenate([g_q, g_k, g_v, lr_pad, d_q, d_k, d_v, c_q], axis=1).astype(BF16)
    w_gates = jnp.concatenate([z_a, z_b, z_c, gate_in], axis=1).astype(BF16)
    return w_branch_in, w_gates


def _gate_weights(w2_l, b2_l, direction):
    w2pad = jnp.zeros((GLR_PAD, GLA_Q), F32)
    w2pad = w2pad.at[direction * GLA_RANK:(direction + 1) * GLA_RANK].set(w2_l[direction].astype(F32))
    return w2pad.astype(BF16), b2_l[direction].reshape(1, GLA_Q).astype(F32)


def _tiles(t):
    tq = min(t // 2, 512)
    qps = 2 if t // tq <= 4 else 1
    band_heads = DIFF_HEADS // 2 if t // tq > 4 else 0
    return dict(tm_in=1024, tm_out=512, tb=min(t, 1024), tq=tq, qps=qps, band_heads=band_heads, tc=min(t, 2048))


def _trunk(x, mem, norm_g, w_in, gla_gate_w2, gla_gate_b, gla_norm_g, diff_lambda, diff_norm_g,
           mem_norm_g, w_mem_kv, w_branch, w_out, final_norm_g):
    b, t, d = x.shape
    tl = _tiles(t)
    x2d = x.reshape(b * t, d)
    mem2d = mem.reshape(b * N_MEM, d)
    for l in range(DEPTH):
        lambda_init = 0.8 - 0.6 * math.exp(-0.3 * l)
        w_branch_in, w_gates = _split_w_in(w_in[l])
        g_in, qkv, cq = branch_inputs(x2d, norm_g[l], w_branch_in, min(tl["tm_in"], b * t))
        g_in, qkv, cq = g_in.reshape(b, t, GLA_IN), qkv.reshape(b, t, 3 * DIFF_QK), cq.reshape(b, t, CROSS_Q)
        kv = norm_matmul(mem2d, mem_norm_g[l], w_mem_kv[l].astype(BF16), BF16, N_MEM, 2 * CROSS_Q)
        kv = kv.reshape(b, N_MEM, 2 * CROSS_Q)
        o_f = gla(g_in, *_gate_weights(gla_gate_w2[l], gla_gate_b[l], 0), False, tl["tb"])
        o_b = gla(g_in, *_gate_weights(gla_gate_w2[l], gla_gate_b[l], 1), True, tl["tb"])
        if tl["band_heads"]:
            hb = tl["band_heads"]
            o_d = diff_attn(qkv, diff_lambda[l], diff_norm_g[l], lambda_init, tl["tq"], 1, (0, hb), True)
            o_d = diff_attn(qkv, diff_lambda[l], diff_norm_g[l], lambda_init, tl["tq"], 1, (hb, DIFF_HEADS),
                            prev=o_d)
        else:
            o_d = diff_attn(qkv, diff_lambda[l], diff_norm_g[l], lambda_init, tl["tq"], tl["qps"])
        o_c = cross_attn(cq, kv, tl["tc"])
        x2d = merge(x2d, o_f.reshape(b * t, GLA_V), o_b.reshape(b * t, GLA_V), o_d.reshape(b * t, DIFF_V),
                    o_c.reshape(b * t, CROSS_Q), norm_g[l], w_gates, gla_norm_g[l], w_branch[l].astype(BF16),
                    w_out[l].astype(BF16), final_norm_g, l == DEPTH - 1, tl["tm_out"])
    return x2d.reshape(b, t, d)


def kernel(x_prompt, x_sample, mem_prompt, mem_sample, norm_g, w_in, gla_gate_w2, gla_gate_b, gla_norm_g,
           diff_lambda, diff_norm_g, mem_norm_g, w_mem_kv, w_branch, w_out, final_norm_g):
    params = (norm_g, w_in, gla_gate_w2, gla_gate_b, gla_norm_g, diff_lambda, diff_norm_g, mem_norm_g,
              w_mem_kv, w_branch, w_out, final_norm_g)
    return (_trunk(x_prompt, mem_prompt, *params), _trunk(x_sample, mem_sample, *params))
```

```python
import functools
import math

import numpy as np
import jax
import jax.numpy as jnp
from jax import lax
from jax.experimental import pallas as pl
from jax.experimental.pallas import tpu as pltpu

F32 = jnp.float32
BF16 = jnp.bfloat16

D_MODEL = 1024
DEPTH = 4
N_MEM = 256
BRANCH_W = 512
N_BRANCH = 3
EPS = 1e-6
GLA_HEADS = 4
GLA_DK = 64
GLA_DV = 128
GLA_RANK = 16
GLA_TAU = 16.0
GLA_Q = GLA_HEADS * GLA_DK
GLA_V = GLA_HEADS * GLA_DV
DIFF_HEADS = 4
DIFF_HD = 64
DIFF_DV = 2 * DIFF_HD
DIFF_QK = DIFF_HEADS * 2 * DIFF_HD
DIFF_V = DIFF_HEADS * DIFF_DV
CROSS_HEADS = 4
CROSS_HD = 128
CROSS_Q = CROSS_HEADS * CROSS_HD
SPLITS = (GLA_Q, GLA_Q, GLA_V, 2 * GLA_RANK, BRANCH_W,
          DIFF_QK, DIFF_QK, DIFF_V, BRANCH_W,
          CROSS_Q, BRANCH_W, N_BRANCH * D_MODEL)

LANES = 128
LOG2E = 1.4426950408889634
EXP2_ZERO_BELOW = -150.0
SUM_SLACK = 0.01
VMEM_LIMIT = 56 * 1024 * 1024
GLA_CHUNK = 128
GLA_LEVELS = 7
GLR_PAD = LANES
GLA_IN = 2 * GLA_Q + GLA_V + GLR_PAD

_NT = (((1,), (1,)), ((), ()))
_TN = (((0,), (0,)), ((), ()))


def _cparams(sem):
    return pltpu.CompilerParams(dimension_semantics=sem, vmem_limit_bytes=VMEM_LIMIT)


def _resident(shape):
    return pl.BlockSpec(shape, lambda *_: (0,) * len(shape), pipeline_mode=pl.Buffered(1))


def _norm_matmul_kernel(x_ref, g_ref, w_ref, o_ref, h_ref):
    @pl.when(pl.program_id(1) == 0)
    def _():
        x = x_ref[...]
        y = x * lax.rsqrt(jnp.mean(x * x, axis=-1, keepdims=True) + EPS)
        h_ref[...] = (y * g_ref[...]).astype(BF16)

    o_ref[...] = jnp.dot(h_ref[...], w_ref[...], preferred_element_type=F32).astype(o_ref.dtype)


def norm_matmul(x2d, g, w_bf16, out_dtype, tm, tn):
    m, d = x2d.shape
    n = w_bf16.shape[1]
    return pl.pallas_call(
        _norm_matmul_kernel,
        out_shape=jax.ShapeDtypeStruct((m, n), out_dtype),
        grid=(m // tm, n // tn),
        in_specs=[pl.BlockSpec((tm, d), lambda i, j: (i, 0)),
                  pl.BlockSpec((1, d), lambda i, j: (0, 0)),
                  pl.BlockSpec((d, tn), lambda i, j: (0, j))],
        out_specs=pl.BlockSpec((tm, tn), lambda i, j: (i, j)),
        scratch_shapes=[pltpu.VMEM((tm, d), BF16)],
        compiler_params=_cparams(("parallel", "arbitrary")),
        name="norm_matmul",
    )(x2d, g.reshape(1, d).astype(F32), w_bf16)


def _proj_kernel(x_ref, g_ref, w_ref, gla_ref, qkv_ref, cq_ref):
    x = x_ref[...]
    h = ((x * lax.rsqrt(jnp.mean(x * x, axis=-1, keepdims=True) + EPS)) * g_ref[...]).astype(BF16)
    c1, c2 = GLA_IN, GLA_IN + 3 * DIFF_QK
    gla_ref[...] = jnp.dot(h, w_ref[:, 0:c1], preferred_element_type=F32)
    qkv_ref[...] = jnp.dot(h, w_ref[:, c1:c2], preferred_element_type=F32).astype(BF16)
    cq_ref[...] = jnp.dot(h, w_ref[:, c2:c2 + CROSS_Q], preferred_element_type=F32).astype(BF16)


def branch_inputs(x2d, g, w_bf16, tm):
    m, d = x2d.shape
    row = lambda w: pl.BlockSpec((tm, w), lambda i: (i, 0))
    return pl.pallas_call(
        _proj_kernel,
        out_shape=(jax.ShapeDtypeStruct((m, GLA_IN), F32),
                   jax.ShapeDtypeStruct((m, 3 * DIFF_QK), BF16),
                   jax.ShapeDtypeStruct((m, CROSS_Q), BF16)),
        grid=(m // tm,),
        in_specs=[row(d), _resident((1, d)), _resident(w_bf16.shape)],
        out_specs=(row(GLA_IN), row(3 * DIFF_QK), row(CROSS_Q)),
        compiler_params=_cparams(("parallel",)),
        name="branch_inputs",
    )(x2d, g.reshape(1, d).astype(F32), w_bf16)


def _gla_constants(reverse):
    c = GLA_CHUNK
    t = np.arange(c)[:, None]
    u = np.arange(c)[None, :]
    start, end, total = (u <= t), (u > t), np.ones((c, c), bool)
    levels, masks = [], [(t == u)]
    for lvl in range(1, GLA_LEVELS + 1):
        bs, half = 1 << lvl, 1 << (lvl - 1)
        boundary = (t // bs) * bs + half - 1
        right = (t % bs) >= half
        levels.append(np.where(right, (u > boundary) & (u <= t), (u > t) & (u <= boundary)))
        masks.append(((u // bs) == (t // bs)) & right & ((u % bs) < half))
    flip = (lambda m: m[::-1, ::-1]) if reverse else (lambda m: m)
    mall_t = np.concatenate([flip(m).T for m in [start, end] + levels + [total]], axis=1).astype(np.float32)
    mask = np.stack([flip(m) for m in masks], axis=0).astype(np.float32)
    mask = np.concatenate([mask, mask], axis=-1)
    return jnp.asarray(mall_t, BF16), jnp.asarray(np.ascontiguousarray(mask), F32)


def _log_decay(logits):
    return (jnp.minimum(logits, 0.0) - jnp.log1p(jnp.exp(-jnp.abs(logits)))) * (1.0 / GLA_TAU)


def _gla_chunk(g_ref, mt_ref, mask_ref, w2t_ref, b2t_ref, o_ref, st_ref, idx):
    c = GLA_CHUNK
    lane_v = lax.broadcasted_iota(jnp.int32, (c, 2 * GLA_DV), 1)
    vlo, vhi = lane_v < GLA_DV, lane_v >= GLA_DV
    drow = lax.broadcasted_iota(jnp.int32, (LANES, c), 0)
    dlo, dhi = drow < GLA_DK, drow >= GLA_DK
    state_mask = ((lax.broadcasted_iota(jnp.int32, (LANES, 2 * GLA_DV), 0) < GLA_DK)
                  == (lax.broadcasted_iota(jnp.int32, (LANES, 2 * GLA_DV), 1) < GLA_DV))
    rows = slice(idx * c, (idx + 1) * c)
    glr_t = g_ref[0, rows, 2 * GLA_Q + GLA_V:GLA_IN].T.astype(BF16)
    log_at = _log_decay(jnp.dot(w2t_ref[...], glr_t, preferred_element_type=F32) + b2t_ref[...])
    e_all = jnp.exp(jnp.dot(log_at.astype(BF16), mt_ref[...], preferred_element_type=F32))
    for p in range(GLA_HEADS // 2):
        q_t = g_ref[0, rows, p * LANES:(p + 1) * LANES].T * (GLA_DK ** -0.5)
        k_t = g_ref[0, rows, GLA_Q + p * LANES:GLA_Q + (p + 1) * LANES].T
        v = g_ref[0, rows, 2 * GLA_Q + p * 2 * GLA_DV:2 * GLA_Q + (p + 1) * 2 * GLA_DV]
        v2 = jnp.concatenate([jnp.where(vlo, v, 0.0), jnp.where(vhi, v, 0.0)], axis=0).astype(BF16)
        e = lambda blk: e_all[p * LANES:(p + 1) * LANES, blk * c:(blk + 1) * c]
        st = st_ref[p]
        o = lax.dot_general((q_t * e(0)).astype(BF16), st.astype(BF16), _TN, preferred_element_type=F32)
        att = jnp.zeros((c, 2 * c), F32)
        for lvl in range(GLA_LEVELS + 1):
            qs, ks = (q_t, k_t) if lvl == 0 else (q_t * e(lvl + 1), k_t * e(lvl + 1))
            k2 = jnp.concatenate([jnp.where(dlo, ks, 0.0), jnp.where(dhi, ks, 0.0)], axis=1)
            a = lax.dot_general(qs.astype(BF16), k2.astype(BF16), _TN, preferred_element_type=F32)
            att = att + a * mask_ref[lvl]
        o = o + jnp.dot(att.astype(BF16), v2, preferred_element_type=F32)
        o_ref[0, rows, p * 2 * GLA_DV:(p + 1) * 2 * GLA_DV] = o
        upd = jnp.dot((k_t * e(1)).astype(BF16), v.astype(BF16), preferred_element_type=F32)
        decay = e(GLA_LEVELS + 2)
        st_ref[p] = st * jnp.concatenate([decay, decay], axis=1) + jnp.where(state_mask, upd, 0.0)


def _gla_kernel(gf_ref, gb_ref, mtf_ref, maskf_ref, w2tf_ref, b2tf_ref, mtb_ref, maskb_ref, w2tb_ref, b2tb_ref,
                of_ref, ob_ref, stf_ref, stb_ref, *, n_chunks):
    @pl.when(pl.program_id(1) == 0)
    def _():
        stf_ref[...] = jnp.zeros_like(stf_ref)
        stb_ref[...] = jnp.zeros_like(stb_ref)

    for ci in range(n_chunks):
        _gla_chunk(gf_ref, mtf_ref, maskf_ref, w2tf_ref, b2tf_ref, of_ref, stf_ref, ci)
        _gla_chunk(gb_ref, mtb_ref, maskb_ref, w2tb_ref, b2tb_ref, ob_ref, stb_ref, n_chunks - 1 - ci)


def gla(g_in, w2pads, b2s, tb):
    b, t, _ = g_in.shape
    nblk = t // tb
    fmap = lambda bi, i: (bi, i, 0)
    bmap = lambda bi, i: (bi, nblk - 1 - i, 0)
    consts = ()
    for direction in range(2):
        mall_t, mask = _gla_constants(direction == 1)
        b2t = jnp.broadcast_to(b2s[direction].reshape(GLA_Q, 1), (GLA_Q, GLA_CHUNK))
        consts += (mall_t, mask, w2pads[direction].T, b2t)
    state = pltpu.VMEM((GLA_HEADS // 2, LANES, 2 * GLA_DV), F32)
    return pl.pallas_call(
        functools.partial(_gla_kernel, n_chunks=tb // GLA_CHUNK),
        out_shape=(jax.ShapeDtypeStruct((b, t, GLA_V), F32), jax.ShapeDtypeStruct((b, t, GLA_V), F32)),
        grid=(b, nblk),
        in_specs=[pl.BlockSpec((1, tb, GLA_IN), fmap), pl.BlockSpec((1, tb, GLA_IN), bmap)]
        + [_resident(a.shape) for a in consts],
        out_specs=(pl.BlockSpec((1, tb, GLA_V), fmap), pl.BlockSpec((1, tb, GLA_V), bmap)),
        scratch_shapes=[state, state],
        compiler_params=_cparams(("parallel", "arbitrary")),
        name="gla",
    )(g_in, g_in, *consts)


def _bf16_terms(x, n=3):
    terms, rest = [], np.asarray(x, np.float64)
    for _ in range(n):
        bits = rest.astype(np.float32).view(np.uint32)
        bits = (bits + np.uint32(0x7FFF) + ((bits >> np.uint32(16)) & np.uint32(1))) & np.uint32(0xFFFF0000)
        term = bits.view(np.float32)
        terms.append(term)
        rest = rest - term.astype(np.float64)
    return terms


def _alibi_constants(t, tq):
    slopes = LOG2E * 2.0 ** (-8.0 * (np.arange(DIFF_HEADS, dtype=np.float64) + 1.0) / DIFF_HEADS)
    pos = np.arange(t)
    parts = ((pos // LANES) * LANES, pos % LANES)
    n = 3 * len(parts)
    kaug = np.zeros((DIFF_HEADS, t, LANES), np.float32)
    qaug = np.zeros((DIFF_HEADS, t, LANES), np.float32)
    kaug[:, :, n:2 * n] = 1.0
    qaug[:, :, 0:n] = 1.0
    for h in range(DIFF_HEADS):
        for j, part in enumerate(parts):
            for i, term in enumerate(_bf16_terms(slopes[h] * part)):
                kaug[h, :, 3 * j + i] = -term
                qaug[h, :, n + 3 * j + i] = term
    i = np.arange(tq)
    dtab = -slopes[:, None, None] * np.abs(i[:, None] - i[None, :])[None]
    dtab = np.concatenate([dtab, dtab], axis=1)
    kaug_t = kaug.reshape(DIFF_HEADS, t // tq, tq, LANES).transpose(0, 1, 3, 2)
    slope_rows = np.broadcast_to(slopes[:, None, None], (DIFF_HEADS, 1, LANES))
    return jnp.asarray(kaug_t, BF16), jnp.asarray(qaug, BF16), jnp.asarray(dtab, F32), jnp.asarray(slope_rows, F32)


def _diff_kernel(q_ref, qn_ref, qf_ref, k_ref, v_ref, kaug_ref, qaug_ref, dtab_ref, slope_ref, lp_ref, gn_ref,
                 *rest, tq, nq, qps, banded, lambda_init):
    o_ref, kp_ref, vp_ref, lhs_ref, acc_ref, s_ref, reach_ref = rest[-7:]
    step = pl.program_id(2)
    nk = nq
    assert nk % 2 == 0
    lane = lax.broadcasted_iota(jnp.int32, (tq, LANES), 1)

    def stacked_q(q):
        qs = q * (LOG2E * DIFF_HD ** -0.5)
        return jnp.concatenate([jnp.where(lane < DIFF_HD, qs, 0.0), jnp.where(lane >= DIFF_HD, qs, 0.0)],
                               axis=0).astype(BF16)

    def key_rows(kj):
        return pl.ds(pl.multiple_of(kj * tq, tq), tq)

    def diagonal_scores(q, kj):
        lhs = jnp.concatenate([stacked_q(q), jnp.zeros((2 * tq, LANES), BF16)], axis=1)
        return jnp.dot(lhs, kp_ref[kj], preferred_element_type=F32) + dtab_ref[0]

    def half_sums(x):
        half = (lax.broadcasted_iota(jnp.int32, (LANES, LANES), 0) < DIFF_HD) == (
            lax.broadcasted_iota(jnp.int32, (LANES, LANES), 1) < DIFF_HD)
        return jnp.dot(x.astype(BF16), jnp.where(half, 1.0, 0.0).astype(BF16), preferred_element_type=F32)

    def overall(reduce, x):
        return reduce(reduce(x, axis=0, keepdims=True), axis=1, keepdims=True)

    def reach_blocks():
        qs = (qf_ref[0] * (LOG2E * DIFF_HD ** -0.5)).astype(F32)
        ks = k_ref[0].astype(F32)
        bound = (1.0 + 2.0 * SUM_SLACK) * jnp.sqrt(
            overall(jnp.max, half_sums(qs * qs)) * overall(jnp.max, half_sums(ks * ks)))
        m_low = overall(jnp.min, half_sums(qs * ks) - SUM_SLACK * half_sums(jnp.abs(qs * ks)) - 0.1)
        blocks = ((bound - m_low - EXP2_ZERO_BELOW) / slope_ref[0, :, 0:1] - 1.0) * (1.0 / tq)
        return jnp.where(blocks < nk, jnp.floor(blocks) + 1.0, float(nk)).astype(jnp.int32)[0, 0]

    def new_head():
        for kb in range(nk):
            kp_ref[kb, 0:LANES, :] = k_ref[0, kb * tq:(kb + 1) * tq, :].T
            kp_ref[kb, LANES:2 * LANES, :] = kaug_ref[0, kb]
        vp_ref[:, 0:LANES] = v_ref[0]
        vp_ref[:, LANES:2 * LANES] = jnp.ones((vp_ref.shape[0], LANES), BF16)
        s_ref[0] = diagonal_scores(q_ref[0, 0:tq, :], 0)
        if banded:
            reach_ref[0] = reach_blocks()

    if nq == qps:
        new_head()
    else:
        pl.when(step == 0)(new_head)

    lp = lp_ref[...]
    lam = (jnp.exp(jnp.sum(lp[0:1] * lp[1:2], axis=-1, keepdims=True))
           - jnp.exp(jnp.sum(lp[2:3] * lp[3:4], axis=-1, keepdims=True)) + lambda_init)

    for j in range(qps):
        qi = step * qps + j
        rows_j = slice(j * tq, (j + 1) * tq)
        lhs, acc = lhs_ref.at[j], acc_ref.at[j]
        qa = jnp.concatenate([qaug_ref[0, rows_j, :], qaug_ref[0, rows_j, :]], axis=0)
        lhs[:, :, 0:LANES] = jnp.broadcast_to(stacked_q(q_ref[0, rows_j, :])[None], (2, 2 * tq, LANES))
        lhs[0, :, LANES:2 * LANES] = -qa
        lhs[1, :, LANES:2 * LANES] = qa

        if banded:
            reach = reach_ref[0]
            lo, hi = jnp.maximum(qi - reach, 0), jnp.minimum(qi + reach, nk - 1)
            odd = (hi - lo) % 2 == 0
            grow_hi = odd & (hi < nk - 1)
            lo, hi = jnp.where(odd & ~grow_hi, lo - 1, lo), jnp.where(grow_hi, hi + 1, hi)
            count = hi - lo + 1
        else:
            lo, count = 0, nk

        def accumulate(s, kj, m, acc=acc):
            m_new = jnp.max(s, axis=-1, keepdims=True)
            if m is not None:
                m_new = jnp.maximum(m, m_new)
            pv = jnp.dot(jnp.exp2(s - m_new).astype(BF16), vp_ref[key_rows(kj), :], preferred_element_type=F32)
            acc[...] = pv if m is None else acc[...] * jnp.exp2(m - m_new) + pv
            return m_new

        def block(n, qi=qi, lo=lo):
            return jnp.where(n == 0, qi, jnp.where(lo + n - 1 < qi, lo + n - 1, lo + n))

        def other_scores(n, qi=qi, lhs=lhs, block=block):
            kj = block(n)
            return jnp.dot(lhs[jnp.where(kj < qi, 0, 1)], kp_ref[kj], preferred_element_type=F32)

        def pair(t, m, accumulate=accumulate, block=block, other_scores=other_scores):
            s_ref[0] = other_scores(2 * t)
            m = accumulate(s_ref[1], block(2 * t - 1), m)
            s_ref[1] = other_scores(2 * t + 1)
            return accumulate(s_ref[0], block(2 * t), m)

        s_ref[1] = other_scores(1)
        m = accumulate(s_ref[0], qi, None)
        if banded:
            m = lax.fori_loop(1, count // 2, pair, m)
        else:
            m = lax.fori_loop(1, count // 2, pair, m, unroll=True)
        if j + 1 < qps:
            s_ref[0] = diagonal_scores(q_ref[0, (j + 1) * tq:(j + 2) * tq, :], qi + 1)
        else:
            s_ref[0] = diagonal_scores(qn_ref[0], jnp.minimum(qi + 1, nq - 1))
        accumulate(s_ref[1], block(count - 1), m)

        a1, a2 = acc[0:tq], acc[tq:2 * tq]
        o = a1[:, 0:LANES] / a1[:, LANES:2 * LANES] - lam * (a2[:, 0:LANES] / a2[:, LANES:2 * LANES])
        y = o * lax.rsqrt(jnp.mean(o * o, axis=-1, keepdims=True) + EPS)
        o_ref[0, rows_j, :] = (y * gn_ref[...]) * (1.0 - lambda_init)


def diff_attn(qkv, lam_params, g_norm, lambda_init, tq, qps, heads=(0, DIFF_HEADS), banded=False, prev=None):
    b, t, _ = qkv.shape
    nq = t // tq
    ns = nq // qps
    kaug, qaug, dtab, slope = _alibi_constants(t, tq)
    h_ = DIFF_HEADS
    h0, nh = heads[0], heads[1] - heads[0]
    operands = (qkv, qkv, qkv, qkv, qkv, kaug, qaug, dtab, slope, lam_params.astype(F32),
                g_norm.reshape(1, DIFF_DV).astype(F32))
    extra_specs, aliases = [], {}
    if prev is not None:
        extra_specs, aliases = [pl.BlockSpec(memory_space=pl.ANY)], {len(operands): 0}
        operands = operands + (prev,)
    return pl.pallas_call(
        functools.partial(_diff_kernel, tq=tq, nq=nq, qps=qps, banded=banded, lambda_init=lambda_init),
        out_shape=jax.ShapeDtypeStruct((b, t, DIFF_V), F32),
        grid=(b, nh, ns),
        in_specs=[pl.BlockSpec((1, qps * tq, LANES), lambda bi, h, si: (bi, si, h0 + h)),
                  pl.BlockSpec((1, tq, LANES), lambda bi, h, si: (bi, jnp.minimum((si + 1) * qps, nq - 1), h0 + h)),
                  pl.BlockSpec((1, t, LANES), lambda bi, h, si: (bi, 0, h0 + h)),
                  pl.BlockSpec((1, t, LANES), lambda bi, h, si: (bi, 0, h_ + h0 + h)),
                  pl.BlockSpec((1, t, LANES), lambda bi, h, si: (bi, 0, 2 * h_ + h0 + h)),
                  pl.BlockSpec((1, nq, LANES, tq), lambda bi, h, si: (h0 + h, 0, 0, 0)),
                  pl.BlockSpec((1, qps * tq, LANES), lambda bi, h, si: (h0 + h, si, 0)),
                  pl.BlockSpec((1, 2 * tq, tq), lambda bi, h, si: (h0 + h, 0, 0)),
                  pl.BlockSpec((1, 1, LANES), lambda bi, h, si: (h0 + h, 0, 0)),
                  pl.BlockSpec((4, DIFF_HD), lambda bi, h, si: (0, 0)),
                  pl.BlockSpec((1, DIFF_DV), lambda bi, h, si: (0, 0))] + extra_specs,
        out_specs=pl.BlockSpec((1, qps * tq, LANES), lambda bi, h, si: (bi, si, h0 + h)),
        input_output_aliases=aliases,
        scratch_shapes=[pltpu.VMEM((nq, 2 * LANES, tq), BF16),
                        pltpu.VMEM((t, 2 * LANES), BF16),
                        pltpu.VMEM((qps, 2, 2 * tq, 2 * LANES), BF16),
                        pltpu.VMEM((qps, 2 * tq, 2 * LANES), F32),
                        pltpu.VMEM((2, 2 * tq, tq), F32),
                        pltpu.SMEM((1,), jnp.int32)],
        compiler_params=_cparams(("parallel", "parallel", "arbitrary")),
        name="diff_attn_banded" if banded else "diff_attn",
    )(*operands)


def _cross_kernel(q_ref, kv_ref, o_ref):
    for h in range(CROSS_HEADS):
        qh = q_ref[0, :, h * CROSS_HD:(h + 1) * CROSS_HD]
        kh = kv_ref[0, :, h * CROSS_HD:(h + 1) * CROSS_HD]
        vh = kv_ref[0, :, CROSS_Q + h * CROSS_HD:CROSS_Q + (h + 1) * CROSS_HD]
        s = lax.dot_general(qh, kh, _NT, preferred_element_type=F32) * (CROSS_HD ** -0.5)
        p = jnp.exp(s - jnp.max(s, axis=-1, keepdims=True))
        l = jnp.sum(p, axis=-1, keepdims=True)
        o = jnp.dot(p.astype(BF16), vh, preferred_element_type=F32)
        o_ref[0, :, h * CROSS_HD:(h + 1) * CROSS_HD] = o / l


def cross_attn(q, kv, tq):
    b, t, _ = q.shape
    return pl.pallas_call(
        _cross_kernel,
        out_shape=jax.ShapeDtypeStruct((b, t, CROSS_Q), F32),
        grid=(b, t // tq),
        in_specs=[pl.BlockSpec((1, tq, CROSS_Q), lambda bi, i: (bi, i, 0)),
                  pl.BlockSpec((1, N_MEM, 2 * CROSS_Q), lambda bi, i: (bi, 0, 0))],
        out_specs=pl.BlockSpec((1, tq, CROSS_Q), lambda bi, i: (bi, i, 0)),
        compiler_params=_cparams(("parallel", "parallel")),
        name="cross_attn",
    )(q, kv)


def _silu(z):
    return z / (1.0 + jnp.exp(-z))


def _sigmoid(z):
    return 1.0 / (1.0 + jnp.exp(-z))


def _merge_kernel(x_ref, of_ref, ob_ref, od_ref, oc_ref, ng_ref, wg_ref, gn_ref, wb_ref, wo_ref, fg_ref, o_ref,
                  *, final):
    x = x_ref[...]
    h = ((x * lax.rsqrt(jnp.mean(x * x, axis=-1, keepdims=True) + EPS)) * ng_ref[...]).astype(BF16)
    og = of_ref[...] + ob_ref[...]
    parts = []
    for hd in range(GLA_HEADS):
        oh = og[:, hd * GLA_DV:(hd + 1) * GLA_DV]
        parts.append(oh * lax.rsqrt(jnp.mean(oh * oh, axis=-1, keepdims=True) + EPS) * gn_ref[...])
    branches = (jnp.concatenate(parts, axis=1), od_ref[...], oc_ref[...])
    merged = None
    for i in range(N_BRANCH):
        z = jnp.dot(h, wg_ref[:, i * BRANCH_W:(i + 1) * BRANCH_W], preferred_element_type=F32)
        a = (branches[i] * _silu(z)).astype(BF16)
        y = jnp.dot(a, wb_ref[i], preferred_element_type=F32)
        c0 = N_BRANCH * BRANCH_W + i * D_MODEL
        gate = _sigmoid(jnp.dot(h, wg_ref[:, c0:c0 + D_MODEL], preferred_element_type=F32))
        merged = gate * y if merged is None else merged + gate * y
    x = x + jnp.dot(merged.astype(BF16), wo_ref[...], preferred_element_type=F32)
    if final:
        x = (x * lax.rsqrt(jnp.mean(x * x, axis=-1, keepdims=True) + EPS)) * fg_ref[...]
    o_ref[...] = x


def merge(x2d, of, ob, od, oc, norm_g, w_gates, gla_norm_g, wb, wo, final_g, final, tm):
    m, d = x2d.shape
    row = lambda w: pl.BlockSpec((tm, w), lambda i: (i, 0))
    return pl.pallas_call(
        functools.partial(_merge_kernel, final=final),
        out_shape=jax.ShapeDtypeStruct((m, d), F32),
        grid=(m // tm,),
        in_specs=[row(d), row(BRANCH_W), row(BRANCH_W), row(BRANCH_W), row(BRANCH_W),
                  _resident((1, d)), _resident(w_gates.shape), _resident((1, GLA_DV)),
                  _resident(wb.shape), _resident(wo.shape), _resident((1, d))],
        out_specs=row(d),
        compiler_params=_cparams(("parallel",)),
        name="merge_final" if final else "merge",
    )(x2d, of, ob, od, oc, norm_g.reshape(1, d).astype(F32), w_gates,
      gla_norm_g.reshape(1, GLA_DV).astype(F32), wb, wo, final_g.reshape(1, d).astype(F32))


def _split_w_in(w_in_l):
    offs = np.concatenate([[0], np.cumsum(np.array(SPLITS))])
    col = lambda i: w_in_l[:, int(offs[i]):int(offs[i + 1])]
    g_q, g_k, g_v, g_lr, z_a, d_q, d_k, d_v, z_b, c_q, z_c, gate_in = (col(i) for i in range(len(SPLITS)))
    lr_pad = jnp.pad(g_lr, ((0, 0), (0, GLR_PAD - 2 * GLA_RANK)))
    w_branch_in = jnp.concatenate([g_q, g_k, g_v, lr_pad, d_q, d_k, d_v, c_q], axis=1).astype(BF16)
    w_gates = jnp.concatenate([z_a, z_b, z_c, gate_in], axis=1).astype(BF16)
    return w_branch_in, w_gates


def _gate_weights(w2_l, b2_l, direction):
    w2pad = jnp.zeros((GLR_PAD, GLA_Q), F32)
    w2pad = w2pad.at[direction * GLA_RANK:(direction + 1) * GLA_RANK].set(w2_l[direction].astype(F32))
    return w2pad.astype(BF16), b2_l[direction].reshape(1, GLA_Q).astype(F32)


def _tiles(t):
    tq = min(t // 2, 512)
    qps = 2 if t // tq <= 4 else 1
    band_heads = DIFF_HEADS // 2 if t // tq > 4 else 0
    return dict(tm_in=1024, tm_out=512, tb=min(t, 1024), tq=tq, qps=qps, band_heads=band_heads, tc=min(t, 4096))


def _trunk(x, mem, norm_g, w_in, gla_gate_w2, gla_gate_b, gla_norm_g, diff_lambda, diff_norm_g,
           mem_norm_g, w_mem_kv, w_branch, w_out, final_norm_g):
    b, t, d = x.shape
    tl = _tiles(t)
    x2d = x.reshape(b * t, d)
    mem2d = mem.reshape(b * N_MEM, d)
    for l in range(DEPTH):
        lambda_init = 0.8 - 0.6 * math.exp(-0.3 * l)
        w_branch_in, w_gates = _split_w_in(w_in[l])
        g_in, qkv, cq = branch_inputs(x2d, norm_g[l], w_branch_in, min(tl["tm_in"], b * t))
        g_in, qkv, cq = g_in.reshape(b, t, GLA_IN), qkv.reshape(b, t, 3 * DIFF_QK), cq.reshape(b, t, CROSS_Q)
        kv = norm_matmul(mem2d, mem_norm_g[l], w_mem_kv[l].astype(BF16), BF16, N_MEM, 2 * CROSS_Q)
        kv = kv.reshape(b, N_MEM, 2 * CROSS_Q)
        gate_w = [_gate_weights(gla_gate_w2[l], gla_gate_b[l], direction) for direction in range(2)]
        o_f, o_b = gla(g_in, [w for w, _ in gate_w], [bias for _, bias in gate_w], tl["tb"])
        if tl["band_heads"]:
            hb = tl["band_heads"]
            o_d = diff_attn(qkv, diff_lambda[l], diff_norm_g[l], lambda_init, tl["tq"], 1, (0, hb), True)
            o_d = diff_attn(qkv, diff_lambda[l], diff_norm_g[l], lambda_init, tl["tq"], 1, (hb, DIFF_HEADS),
                            prev=o_d)
        else:
            o_d = diff_attn(qkv, diff_lambda[l], diff_norm_g[l], lambda_init, tl["tq"], tl["qps"])
        o_c = cross_attn(cq, kv, tl["tc"])
        x2d = merge(x2d, o_f.reshape(b * t, GLA_V), o_b.reshape(b * t, GLA_V), o_d.reshape(b * t, DIFF_V),
                    o_c.reshape(b * t, CROSS_Q), norm_g[l], w_gates, gla_norm_g[l], w_branch[l].astype(BF16),
                    w_out[l].astype(BF16), final_norm_g, l == DEPTH - 1, tl["tm_out"])
    return x2d.reshape(b, t, d)


def kernel(x_prompt, x_sample, mem_prompt, mem_sample, norm_g, w_in, gla_gate_w2, gla_gate_b, gla_norm_g,
           diff_lambda, diff_norm_g, mem_norm_g, w_mem_kv, w_branch, w_out, final_norm_g):
    params = (norm_g, w_in, gla_gate_w2, gla_gate_b, gla_norm_g, diff_lambda, diff_norm_g, mem_norm_g,
              w_mem_kv, w_branch, w_out, final_norm_g)
    return (_trunk(x_prompt, mem_prompt, *params), _trunk(x_sample, mem_sample, *params))
```
